```python
import math
import jax, jax.numpy as jnp
from jax import lax
import numpy as np

D_MODEL = 1024
BATCH = 16
SEQ = 256
DEPTH = 2
DEC_BATCH = 4
DEC_SEQ = 4096
PAST_LEN = 256

GRID_W = 64
N_AB_LAYERS = (DEPTH + 1) // 2
N_C_LAYERS = DEPTH // 2
CONV_CH = D_MODEL // 2
CONV_WIDTH = 31
M_HEADS = 4
M_DK = (D_MODEL // 2) // M_HEADS
M_DV = (D_MODEL // 2) // M_HEADS
M_WIDTH = M_HEADS * M_DV
CHUNK = 64
FORGET_BIAS = 3.0
AB_IN = 2 * CONV_CH + 4 * M_WIDTH + 4 * M_HEADS
HEAD_DIM = 128
N_Q = D_MODEL // HEAD_DIM
N_KV = 2
GROUP = N_Q // N_KV
C_IN = (N_Q + 2 * N_KV) * HEAD_DIM
Q_BLOCK = 128
ROPE_THETA = 10000.0
D_FF = -(-8 * D_MODEL // (3 * 256)) * 256
EPS = 1e-6

kernel_name = 'hybrid_diffusion_conv_mlstm_gqa_step'


def rms_norm(x, g):
    x32 = x.astype(jnp.float32)
    y = x32 * lax.rsqrt(jnp.mean(x32 * x32, axis=-1, keepdims=True) + EPS)
    return (y * g.astype(jnp.float32)).astype(x.dtype)


def layer_norm(x, g, b):
    x32 = x.astype(jnp.float32)
    mu = jnp.mean(x32, axis=-1, keepdims=True)
    var = jnp.mean(jnp.square(x32 - mu), axis=-1, keepdims=True)
    y = (x32 - mu) * lax.rsqrt(var + EPS)
    return (y * g.astype(jnp.float32) + b.astype(jnp.float32)).astype(x.dtype)


def adaln(cond, w, b):
    mod = jax.nn.silu(cond) @ w + b
    return jnp.split(mod[:, None, :], 6, axis=-1)


def modulate(x, g, shift, scale):
    return rms_norm(x, g) * (1 + scale) + shift


def swiglu(h, w_in, w_out):
    gt, up = jnp.split(h @ w_in, 2, axis=-1)
    return (jax.nn.silu(gt) * up) @ w_out


def depthwise_conv(x, w, b):
    y = lax.conv_general_dilated(
        x, w[:, None, :], window_strides=(1,),
        padding=[(CONV_WIDTH // 2, CONV_WIDTH // 2)],
        dimension_numbers=('NWC', 'WIO', 'NWC'),
        feature_group_count=x.shape[-1])
    return y + b


def mlstm_scan(q, k, v, log_i, log_f, state):
    B, S, H, _ = q.shape
    DV = v.shape[-1]
    nc = S // CHUNK

    def chunks(a):
        a = a.reshape((B, nc, CHUNK, H) + a.shape[3:])
        return jnp.moveaxis(a, 3, 2).swapaxes(0, 1)

    mask = jnp.tril(jnp.ones((CHUNK, CHUNK), dtype=bool))

    def step(carry, xs):
        C, n, m = carry
        qc, kc, vc, ic, fc = xs
        b = jnp.cumsum(fc, axis=-1)
        logd = jnp.where(mask, b[..., :, None] - b[..., None, :] + ic[..., None, :], -jnp.inf)
        inter = b + m[..., None]
        m_t = jnp.maximum(inter, jnp.max(logd, axis=-1))
        dmat = jnp.exp(logd - m_t[..., None])
        w_inter = jnp.exp(inter - m_t)
        s = jnp.einsum('bhtd,bhsd->bhts', qc, kc) * dmat
        num = jnp.einsum('bhts,bhsv->bhtv', s, vc) + w_inter[..., None] * jnp.einsum('bhtd,bhdv->bhtv', qc, C)
        den = jnp.sum(s, axis=-1) + w_inter * jnp.einsum('bhtd,bhd->bht', qc, n)
        h = num / jnp.maximum(jnp.abs(den), jnp.exp(-m_t))[..., None]
        b_end = b[..., -1]
        log_w = b_end[..., None] - b + ic
        m_new = jnp.maximum(b_end + m, jnp.max(log_w, axis=-1))
        w = jnp.exp(log_w - m_new[..., None])
        a = jnp.exp(b_end + m - m_new)
        C_new = a[..., None, None] * C + jnp.einsum('bhsd,bhsv->bhdv', kc * w[..., None], vc)
        n_new = a[..., None] * n + jnp.einsum('bhs,bhsd->bhd', w, kc)
        return (C_new, n_new, m_new), h

    state, h = lax.scan(step, state, (chunks(q), chunks(k), chunks(v), chunks(log_i), chunks(log_f)))
    h = jnp.moveaxis(h.swapaxes(0, 1), 2, 3).reshape(B, S, H, DV)
    return h, state


def mlstm_bidir(q, k, v, gates, state_f, state_b):
    i_f, f_f, i_b, f_b = jnp.split(gates, 4, axis=-1)
    h_f, st_f = mlstm_scan(q, k, v, i_f, jax.nn.log_sigmoid(f_f), state_f)
    flip = lambda a: jnp.flip(a, axis=1)
    h_b, st_b = mlstm_scan(flip(q), flip(k), flip(v), flip(i_b), flip(jax.nn.log_sigmoid(f_b)), state_b)
    return h_f + flip(h_b), st_f, st_b


def ab_mixer(h, w_in, b_gate, conv_w, conv_b, ln_g, ln_b, mnorm_g, w_out, state_f, state_b):
    B, S, _ = h.shape
    p = h @ w_in
    cuts = [CONV_CH, 2 * CONV_CH, 2 * CONV_CH + M_WIDTH, 2 * CONV_CH + 2 * M_WIDTH,
            2 * CONV_CH + 3 * M_WIDTH, 2 * CONV_CH + 4 * M_WIDTH]
    ga, gg, q, k, v, o, gates = jnp.split(p, cuts, axis=-1)
    u = ga * jax.nn.sigmoid(gg)
    u = jax.nn.silu(layer_norm(depthwise_conv(u, conv_w, conv_b), ln_g, ln_b))
    f32 = jnp.float32
    q32 = q.reshape(B, S, M_HEADS, M_DK).astype(f32)
    k32 = k.reshape(B, S, M_HEADS, M_DK).astype(f32) * (M_DK ** -0.5)
    v32 = v.reshape(B, S, M_HEADS, M_DV).astype(f32)
    g32 = (gates + b_gate).astype(f32)
    hm, st_f, st_b = mlstm_bidir(q32, k32, v32, g32, state_f, state_b)
    hm = hm * lax.rsqrt(jnp.mean(hm * hm, axis=-1, keepdims=True) + EPS)
    hm = (hm * mnorm_g.astype(f32).reshape(M_HEADS, M_DV)).reshape(B, S, M_WIDTH).astype(h.dtype)
    hm = jax.nn.sigmoid(o) * hm
    out = jnp.concatenate([u, hm], axis=-1) @ w_out
    return out, st_f, st_b


def rope_2d(x):
    S = x.shape[1]
    rows = S // GRID_W
    t_row = jnp.repeat(jnp.arange(rows), GRID_W)
    t_col = jnp.tile(jnp.arange(GRID_W), rows)
    n_freq = HEAD_DIM // 4
    inv = 1.0 / (ROPE_THETA ** (jnp.arange(n_freq, dtype=jnp.float32) / n_freq))

    def rot(xa, pos):
        ang = pos.astype(jnp.float32)[:, None] * inv
        cos = jnp.cos(ang)[None, :, None, :]
        sin = jnp.sin(ang)[None, :, None, :]
        x1, x2 = jnp.split(xa, 2, axis=-1)
        return jnp.concatenate([x1 * cos - x2 * sin, x1 * sin + x2 * cos], axis=-1)

    xr, xc = jnp.split(x.astype(jnp.float32), 2, axis=-1)
    return jnp.concatenate([rot(xr, t_row), rot(xc, t_col)], axis=-1).astype(x.dtype)


def blocked_attention(q, k, v):
    B, Sq = q.shape[:2]
    nb = Sq // Q_BLOCK
    qb = q.reshape(B, nb, Q_BLOCK, N_KV, GROUP, HEAD_DIM).swapaxes(0, 1)
    scale = HEAD_DIM ** -0.5

    def one(qblk):
        s = jnp.einsum('bqhgd,bshd->bhgqs', qblk, k).astype(jnp.float32) * scale
        p = jax.nn.softmax(s, axis=-1).astype(v.dtype)
        return jnp.einsum('bhgqs,bshd->bqhgd', p, v)

    o = lax.map(one, qb)
    return o.swapaxes(0, 1).reshape(B, Sq, N_Q * HEAD_DIM)


def qkv_proj(h, w_in, qg, kg):
    B, S, _ = h.shape
    q, k, v = jnp.split(h @ w_in, [N_Q * HEAD_DIM, (N_Q + N_KV) * HEAD_DIM], axis=-1)
    q = rms_norm(q.reshape(B, S, N_Q, HEAD_DIM), qg)
    k = rms_norm(k.reshape(B, S, N_KV, HEAD_DIM), kg)
    return q, k, v.reshape(B, S, N_KV, HEAD_DIM)


def attn_context(h, w_in, qg, kg, w_out):
    B, S, _ = h.shape
    q, k, v = qkv_proj(h, w_in, qg, kg)
    o = blocked_attention(q.reshape(B, S, N_KV, GROUP, HEAD_DIM), k, v)
    return o @ w_out, k, v


def attn_latent(h, w_in, qg, kg, w_out, k_ctx, v_ctx):
    B, S, _ = h.shape
    q, k, v = qkv_proj(h, w_in, qg, kg)
    q, k = rope_2d(q), rope_2d(k)
    k_all = jnp.concatenate([k, k_ctx.astype(k.dtype)], axis=1)
    v_all = jnp.concatenate([v, v_ctx.astype(v.dtype)], axis=1)
    o = blocked_attention(q.reshape(B, S, N_KV, GROUP, HEAD_DIM), k_all, v_all)
    return o @ w_out


def setup_inputs(seed: int = 0) -> dict:
    key = jax.random.key(seed)
    ks = iter(jax.random.split(key, 40))

    def nrm(shape, scale=1.0):
        return scale * jax.random.normal(next(ks), shape, jnp.float32)

    D = D_MODEL
    gate_offset = jnp.tile(jnp.concatenate([jnp.zeros((M_HEADS,), jnp.float32),
                                            jnp.full((M_HEADS,), FORGET_BIAS, jnp.float32)]), 2)
    return {
        'x_prompt': nrm((BATCH, SEQ, D)),
        'x_sample': nrm((DEC_BATCH, DEC_SEQ, D)),
        'c': nrm((DEC_BATCH, D)),
        'state_mlstm_C': nrm((DEC_BATCH, N_AB_LAYERS, 2, M_HEADS, M_DK, M_DV), 0.5),
        'state_mlstm_n': nrm((DEC_BATCH, N_AB_LAYERS, 2, M_HEADS, M_DK), 0.5),
        'state_mlstm_m': nrm((DEC_BATCH, N_AB_LAYERS, 2, M_HEADS)),
        'cache_k': nrm((DEC_BATCH, N_C_LAYERS, PAST_LEN, N_KV, HEAD_DIM)),
        'cache_v': nrm((DEC_BATCH, N_C_LAYERS, PAST_LEN, N_KV, HEAD_DIM)),
        'c_ctx': nrm((D,)),
        'w_mod': nrm((DEPTH, D, 6 * D), D ** -0.5),
        'b_mod': nrm((DEPTH, 6 * D), 0.02),
        'norm1_g': 1.0 + nrm((DEPTH, D), 0.02),
        'norm2_g': 1.0 + nrm((DEPTH, D), 0.02),
        'w_in_ab': nrm((N_AB_LAYERS, D, AB_IN), D ** -0.5),
        'b_gate_ab': gate_offset + nrm((N_AB_LAYERS, 4 * M_HEADS), 0.1),
        'conv_w': nrm((N_AB_LAYERS, CONV_WIDTH, CONV_CH), CONV_WIDTH ** -0.5),
        'conv_b': nrm((N_AB_LAYERS, CONV_CH), 0.02),
        'conv_ln_g': 1.0 + nrm((N_AB_LAYERS, CONV_CH), 0.02),
        'conv_ln_b': nrm((N_AB_LAYERS, CONV_CH), 0.02),
        'mlstm_norm_g': 1.0 + nrm((N_AB_LAYERS, M_WIDTH), 0.02),
        'w_out_ab': nrm((N_AB_LAYERS, CONV_CH + M_WIDTH, D), (CONV_CH + M_WIDTH) ** -0.5),
        'w_in_c': nrm((N_C_LAYERS, D, C_IN), D ** -0.5),
        'q_norm_g': 1.0 + nrm((N_C_LAYERS, HEAD_DIM), 0.02),
        'k_norm_g': 1.0 + nrm((N_C_LAYERS, HEAD_DIM), 0.02),
        'w_out_c': nrm((N_C_LAYERS, N_Q * HEAD_DIM, D), (N_Q * HEAD_DIM) ** -0.5),
        'w_ffn_in': nrm((DEPTH, D, 2 * D_FF), D ** -0.5),
        'w_ffn_out': nrm((DEPTH, D_FF, D), D_FF ** -0.5),
        'final_norm_g': 1.0 + nrm((D,), 0.02),
    }


def reference(x_prompt, x_sample, c, state_mlstm_C, state_mlstm_n, state_mlstm_m, cache_k, cache_v,
              c_ctx, w_mod, b_mod, norm1_g, norm2_g, w_in_ab, b_gate_ab, conv_w, conv_b, conv_ln_g,
              conv_ln_b, mlstm_norm_g, w_out_ab, w_in_c, q_norm_g, k_norm_g, w_out_c, w_ffn_in,
              w_ffn_out, final_norm_g):
    f32 = jnp.float32
    ctx, lat = x_prompt, x_sample
    Bc = ctx.shape[0]
    zero_state = (jnp.zeros((Bc, M_HEADS, M_DK, M_DV), f32),
                  jnp.zeros((Bc, M_HEADS, M_DK), f32),
                  jnp.zeros((Bc, M_HEADS), f32))
    new_C, new_n, new_m, new_k, new_v = [], [], [], [], []
    for l in range(DEPTH):
        sh1c, sc1c, g1c, sh2c, sc2c, g2c = adaln(c_ctx[None, :], w_mod[l], b_mod[l])
        sh1, sc1, g1, sh2, sc2, g2 = adaln(c, w_mod[l], b_mod[l])
        hc = modulate(ctx, norm1_g[l], sh1c, sc1c)
        hl = modulate(lat, norm1_g[l], sh1, sc1)
        j = l // 2
        if l % 2 == 0:
            ab_w = (w_in_ab[j], b_gate_ab[j], conv_w[j], conv_b[j], conv_ln_g[j], conv_ln_b[j],
                    mlstm_norm_g[j], w_out_ab[j])
            oc, sf, sb = ab_mixer(hc, *ab_w, zero_state, zero_state)
            new_C.append(jnp.stack([sf[0], sb[0]], axis=1))
            new_n.append(jnp.stack([sf[1], sb[1]], axis=1))
            new_m.append(jnp.stack([sf[2], sb[2]], axis=1))
            lat_f = (state_mlstm_C[:, j, 0].astype(f32), state_mlstm_n[:, j, 0].astype(f32),
                     state_mlstm_m[:, j, 0].astype(f32))
            lat_b = (state_mlstm_C[:, j, 1].astype(f32), state_mlstm_n[:, j, 1].astype(f32),
                     state_mlstm_m[:, j, 1].astype(f32))
            ol, _, _ = ab_mixer(hl, *ab_w, lat_f, lat_b)
        else:
            oc, kc, vc = attn_context(hc, w_in_c[j], q_norm_g[j], k_norm_g[j], w_out_c[j])
            new_k.append(kc)
            new_v.append(vc)
            ol = attn_latent(hl, w_in_c[j], q_norm_g[j], k_norm_g[j], w_out_c[j],
                             cache_k[:, j], cache_v[:, j])
        ctx = ctx + g1c * oc
        lat = lat + g1 * ol
        ctx = ctx + g2c * swiglu(modulate(ctx, norm2_g[l], sh2c, sc2c), w_ffn_in[l], w_ffn_out[l])
        lat = lat + g2 * swiglu(modulate(lat, norm2_g[l], sh2, sc2), w_ffn_in[l], w_ffn_out[l])
    y_prompt = rms_norm(ctx, final_norm_g)
    y_sample = rms_norm(lat, final_norm_g)
    dt = x_prompt.dtype
    return (y_prompt, y_sample,
            jnp.stack(new_C, axis=1).astype(dt), jnp.stack(new_n, axis=1).astype(dt),
            jnp.stack(new_m, axis=1).astype(dt), jnp.stack(new_k, axis=1), jnp.stack(new_v, axis=1))
```

```python
import functools

import jax
import jax.numpy as jnp
from jax import lax
from jax.experimental import pallas as pl
from jax.experimental.pallas import tpu as pltpu

F32 = jnp.float32
BF16 = jnp.bfloat16

D_MODEL = 1024
N_GROUPS = 5
GROUP_TOKENS = 4096
CTX_SEQ = 256
CONV_CH = 512
CONV_WIDTH = 31
M_HEADS = 4
M_DK = 128
M_WIDTH = 512
HEAD_DIM = 128
N_Q = 8
N_KV = 2
GROUP = N_Q // N_KV
PAST_LEN = 256
GRID_W = 64
ROPE_THETA = 10000.0
D_FF = 2816
EPS = 1e-6

CHUNK = 256
TOKEN_TILE = 512
SEQ_TILE = 256
FFN_CHUNK = 1408
HALO = 16
VMEM_LIMIT = 56 * 1024 * 1024

_ARB2 = pltpu.CompilerParams(dimension_semantics=("arbitrary", "arbitrary"),
                             vmem_limit_bytes=VMEM_LIMIT)
_ARB3 = pltpu.CompilerParams(dimension_semantics=("arbitrary", "arbitrary", "arbitrary"),
                             vmem_limit_bytes=VMEM_LIMIT)


def _rms(x):
    return x * lax.rsqrt(jnp.mean(x * x, axis=-1, keepdims=True) + EPS)


def _modulated(x, gain, shift, scale):
    return (_rms(x) * gain) * (1.0 + scale) + shift


def _sigmoid(x):
    return 1.0 / (1.0 + jnp.exp(-x))


def _log_sigmoid(x):
    return jnp.minimum(x, 0.0) - jnp.log1p(jnp.exp(-jnp.abs(x)))


def _dot(a, b):
    return jnp.dot(a, b, preferred_element_type=F32)


def _dot_nt(a, b):
    return lax.dot_general(a, b, (((1,), (1,)), ((), ())), preferred_element_type=F32)


def _dot_tn(a, b):
    return lax.dot_general(a, b, (((0,), (0,)), ((), ())), preferred_element_type=F32)


def _dot_exact(a, b):
    return jnp.dot(a, b, preferred_element_type=F32, precision=lax.Precision.HIGHEST)


def _mod_kernel(cond_ref, w_ref, b_ref, o_ref):
    c = cond_ref[...]
    s = (c * _sigmoid(c)).astype(BF16)
    o_ref[...] = _dot(s, w_ref[...].astype(BF16)) + b_ref[...]


def _adaln_all(cond8, w_mod, b_mod):
    depth, d, n = w_mod.shape
    tn = 1024
    return pl.pallas_call(
        _mod_kernel,
        grid=(depth, n // tn),
        in_specs=[
            pl.BlockSpec((8, d), lambda l, j: (0, 0)),
            pl.BlockSpec((None, d, tn), lambda l, j: (l, 0, j)),
            pl.BlockSpec((None, 1, tn), lambda l, j: (l, 0, j)),
        ],
        out_specs=pl.BlockSpec((None, 8, tn), lambda l, j: (l, 0, j)),
        out_shape=jax.ShapeDtypeStruct((depth, 8, n), F32),
        compiler_params=_ARB2,
        name="adaln_mod",
    )(cond8, w_mod, b_mod.reshape(depth, 1, n))


def _ab_in_kernel(x_ref, mod_ref, g_ref, w_ref, wg_ref, wgt_ref, bgr_ref, bgc_ref,
                  u_ref, q_ref, k_ref, v_ref, o_ref, gc_ref, gr_ref):
    h = _modulated(x_ref[...], g_ref[...], mod_ref[0:1, :], mod_ref[1:2, :])
    hb = h.astype(BF16)

    def proj(c):
        return _dot(hb, w_ref[:, c * CONV_CH:(c + 1) * CONV_CH])

    u_ref[...] = proj(0) * _sigmoid(proj(1))
    q_ref[...] = proj(2)
    k_ref[...] = proj(3) * (M_DK ** -0.5)
    v_ref[...] = proj(4)
    o_ref[...] = proj(5)
    gc = _dot(hb, wg_ref[...]) + bgr_ref[...]
    lane = lax.broadcasted_iota(jnp.int32, gc.shape, 1)
    gc_ref[...] = jnp.where((lane & M_HEADS) != 0, _log_sigmoid(gc), gc)
    gr = _dot_nt(wgt_ref[...], hb) + bgc_ref[...]
    row = lax.broadcasted_iota(jnp.int32, gr.shape, 0)
    gr_ref[...] = jnp.where((row & M_HEADS) != 0, _log_sigmoid(gr), gr)


def _ab_in(x, mod, gain, w_main, w_gate, w_gate_t, b_gate_row, b_gate_col):
    tm = TOKEN_TILE
    nt = GROUP_TOKENS // tm
    tok = lambda width: pl.BlockSpec((None, tm, width), lambda g, t: (g, t, 0))
    const = lambda a: pl.BlockSpec(a.shape, lambda g, t: (0,) * a.ndim)
    sds = lambda width: jax.ShapeDtypeStruct((N_GROUPS, GROUP_TOKENS, width), F32)
    return pl.pallas_call(
        _ab_in_kernel,
        grid=(N_GROUPS, nt),
        in_specs=[
            tok(D_MODEL),
            pl.BlockSpec((None, 6, D_MODEL), lambda g, t: (g, 0, 0)),
            const(gain), const(w_main), const(w_gate), const(w_gate_t),
            const(b_gate_row), const(b_gate_col),
        ],
        out_specs=[tok(CONV_CH), tok(M_WIDTH), tok(M_WIDTH), tok(M_WIDTH), tok(M_WIDTH),
                   tok(128),
                   pl.BlockSpec((None, 16, tm), lambda g, t: (g, 0, t))],
        out_shape=[sds(CONV_CH), sds(M_WIDTH), sds(M_WIDTH), sds(M_WIDTH), sds(M_WIDTH),
                   sds(128),
                   jax.ShapeDtypeStruct((N_GROUPS, 16, GROUP_TOKENS), F32)],
        compiler_params=_ARB2,
        name="ab_in_proj",
    )(x, mod, gain, w_main, w_gate, w_gate_t, b_gate_row, b_gate_col)


def _mlstm_direction(q_ref, k_ref, v_ref, gc_ref, gr_ref, h_ref, c_s, n_s, m_s, d):
    L = q_ref.shape[0]
    r = lax.broadcasted_iota(jnp.int32, (L, L), 0)
    c = lax.broadcasted_iota(jnp.int32, (L, L), 1)
    lower = r >= c
    upper = r <= c
    if d == 0:
        mask, cum_c, cum_r, end = lower, lower.astype(F32), upper.astype(F32), L - 1
    else:
        mask, cum_c, cum_r, end = upper, upper.astype(F32), lower.astype(F32), 0
    gc = gc_ref[...]
    gr = gr_ref[...]
    bc = _dot_exact(cum_c, gc)
    br = _dot_exact(gr, cum_r)
    for h in range(M_HEADS):
        s_idx = d * M_HEADS + h
        icol = 2 * M_HEADS * d + h
        fcol = icol + M_HEADS
        lanes = slice(h * M_DK, (h + 1) * M_DK)
        b_c = bc[:, fcol:fcol + 1]
        b_r = br[fcol:fcol + 1, :]
        i_c = gc[:, icol:icol + 1]
        i_r = gr[icol:icol + 1, :]
        m_prev = m_s[s_idx:s_idx + 1, 0:1]
        c_prev = c_s[s_idx]
        n_prev = n_s[s_idx:s_idx + 1, :]
        q = q_ref[:, lanes]
        k = k_ref[:, lanes]
        v = v_ref[:, lanes]
        qb = q.astype(BF16)
        vb = v.astype(BF16)

        logd = jnp.where(mask, b_c - b_r + i_r, -jnp.inf)
        inter = b_c + m_prev
        m_t = jnp.maximum(inter, jnp.max(logd, axis=-1, keepdims=True))
        dmat = jnp.exp(logd - m_t)
        w_inter = jnp.exp(inter - m_t)
        s = _dot_nt(qb, k.astype(BF16)) * dmat
        num = _dot(s.astype(BF16), vb) + w_inter * _dot(qb, c_prev.astype(BF16))
        den = (jnp.sum(s, axis=-1, keepdims=True)
               + w_inter * jnp.sum(q * n_prev, axis=-1, keepdims=True))
        h_ref[:, lanes] = num / jnp.maximum(jnp.abs(den), jnp.exp(-m_t))

        b_end = b_c[end:end + 1, :]
        log_w = b_end - b_c + i_c
        m_new = jnp.maximum(b_end + m_prev, jnp.max(log_w, axis=0, keepdims=True))
        kw = k * jnp.exp(log_w - m_new)
        a = jnp.exp(b_end + m_prev - m_new)
        c_s[s_idx] = a * c_prev + _dot_tn(kw.astype(BF16), vb)
        n_s[s_idx:s_idx + 1, :] = a * n_prev + jnp.sum(kw, axis=0, keepdims=True)
        m_s[s_idx:s_idx + 1, :] = jnp.broadcast_to(m_new, (1, 128))


def _mlstm_kernel(qf_ref, kf_ref, vf_ref, gcf_ref, grf_ref,
                  qb_ref, kb_ref, vb_ref, gcb_ref, grb_ref,
                  c0_ref, n0_ref, m0_ref,
                  hf_ref, hb_ref, co_ref, no_ref, mo_ref,
                  c_s, n_s, m_s):
    g = pl.program_id(0)
    c = pl.program_id(1)

    @pl.when(g == 0)
    def _():
        c_s[...] = jnp.zeros_like(c_s)
        n_s[...] = jnp.zeros_like(n_s)
        m_s[...] = jnp.zeros_like(m_s)

    @pl.when((g > 0) & (c == 0))
    def _():
        c_s[...] = c0_ref[...]
        n_s[...] = n0_ref[...]
        m_s[...] = m0_ref[...]

    _mlstm_direction(qf_ref, kf_ref, vf_ref, gcf_ref, grf_ref, hf_ref, c_s, n_s, m_s, 0)
    _mlstm_direction(qb_ref, kb_ref, vb_ref, gcb_ref, grb_ref, hb_ref, c_s, n_s, m_s, 1)

    @pl.when(g == 0)
    def _():
        co_ref[...] = c_s[...]
        no_ref[...] = n_s[...]
        mo_ref[...] = m_s[...]


def _mlstm(q, k, v, gc, gr, c0, n0, m0):
    L = CHUNK
    nc = GROUP_TOKENS // L
    n_ctx = GROUP_TOKENS // CTX_SEQ
    assert CTX_SEQ == L and n_ctx == nc
    n_streams = 2 * M_HEADS
    fwd = lambda g, c: c
    bwd = lambda g, c: jnp.where(g == 0, c, nc - 1 - c)
    tok = lambda width, pos: pl.BlockSpec((None, L, width), lambda g, c: (g, pos(g, c), 0))
    row = lambda pos: pl.BlockSpec((None, 16, L), lambda g, c: (g, 0, pos(g, c)))
    lat = lambda g, c: jnp.maximum(g - 1, 0)
    ctx = lambda g, c: jnp.where(g == 0, c, n_ctx - 1)
    return pl.pallas_call(
        _mlstm_kernel,
        grid=(N_GROUPS, nc),
        in_specs=[
            tok(M_WIDTH, fwd), tok(M_WIDTH, fwd), tok(M_WIDTH, fwd), tok(128, fwd), row(fwd),
            tok(M_WIDTH, bwd), tok(M_WIDTH, bwd), tok(M_WIDTH, bwd), tok(128, bwd), row(bwd),
            pl.BlockSpec((None, n_streams, M_DK, M_DK), lambda g, c: (lat(g, c), 0, 0, 0)),
            pl.BlockSpec((None, n_streams, M_DK), lambda g, c: (lat(g, c), 0, 0)),
            pl.BlockSpec((None, n_streams, 128), lambda g, c: (lat(g, c), 0, 0)),
        ],
        out_specs=[
            tok(M_WIDTH, fwd), tok(M_WIDTH, bwd),
            pl.BlockSpec((None, n_streams, M_DK, M_DK), lambda g, c: (ctx(g, c), 0, 0, 0)),
            pl.BlockSpec((None, n_streams, M_DK), lambda g, c: (ctx(g, c), 0, 0)),
            pl.BlockSpec((None, n_streams, 128), lambda g, c: (ctx(g, c), 0, 0)),
        ],
        out_shape=[
            jax.ShapeDtypeStruct((N_GROUPS, GROUP_TOKENS, M_WIDTH), F32),
            jax.ShapeDtypeStruct((N_GROUPS, GROUP_TOKENS, M_WIDTH), F32),
            jax.ShapeDtypeStruct((n_ctx, n_streams, M_DK, M_DK), F32),
            jax.ShapeDtypeStruct((n_ctx, n_streams, M_DK), F32),
            jax.ShapeDtypeStruct((n_ctx, n_streams, 128), F32),
        ],
        scratch_shapes=[
            pltpu.VMEM((n_streams, M_DK, M_DK), F32),
            pltpu.VMEM((n_streams, M_DK), F32),
            pltpu.VMEM((n_streams, 128), F32),
        ],
        compiler_params=_ARB2,
        name="mlstm_scan",
    )(q, k, v, gc, gr, q, k, v, gc, gr, c0, n0, m0)


def _ab_out_kernel(up_ref, uc_ref, un_ref, hf_ref, hb_ref, o_ref, x_ref, mod_ref,
                   cw_ref, cb_ref, lg_ref, lb_ref, mg_ref, wu_ref, wm_ref,
                   out_ref, ext_ref, conv_ref):
    g = pl.program_id(0)
    t = pl.program_id(1)
    nt = pl.num_programs(1)
    T = uc_ref.shape[0]
    prev_on = jnp.where((g > 0) & (t > 0), 1.0, 0.0).astype(F32)
    next_on = jnp.where((g > 0) & (t < nt - 1), 1.0, 0.0).astype(F32)
    ext_ref[0:HALO, :] = up_ref[...] * prev_on
    ext_ref[HALO:HALO + T, :] = uc_ref[...]
    ext_ref[HALO + T:HALO + T + HALO, :] = un_ref[...] * next_on

    rb = 32
    off = HALO - CONV_WIDTH // 2
    for r0 in range(0, T, rb):
        acc = jnp.broadcast_to(cb_ref[...], (rb, CONV_CH))
        for j in range(CONV_WIDTH):
            acc = acc + cw_ref[j:j + 1, :] * ext_ref[r0 + off + j:r0 + off + j + rb, :]
        conv_ref[r0:r0 + rb, :] = acc

    y = conv_ref[...]
    mu = jnp.mean(y, axis=-1, keepdims=True)
    yc = y - mu
    var = jnp.mean(yc * yc, axis=-1, keepdims=True)
    y = yc * lax.rsqrt(var + EPS) * lg_ref[...] + lb_ref[...]
    u = y * _sigmoid(y)

    hm = hf_ref[...] + hb_ref[...]
    hm = jnp.concatenate(
        [_rms(hm[:, h * M_DK:(h + 1) * M_DK]) for h in range(M_HEADS)], axis=-1)
    hm = _sigmoid(o_ref[...]) * (hm * mg_ref[...])
    out = _dot(u.astype(BF16), wu_ref[...]) + _dot(hm.astype(BF16), wm_ref[...])
    out_ref[...] = x_ref[...] + mod_ref[2:3, :] * out


def _ab_out(u, hf, hb, o, x, mod, conv_w, conv_b, ln_g, ln_b, mnorm_g, w_u, w_m):
    T = SEQ_TILE
    nt = GROUP_TOKENS // T
    per = T // HALO
    nh = GROUP_TOKENS // HALO
    tok = lambda width: pl.BlockSpec((None, T, width), lambda g, t: (g, t, 0))
    const = lambda a: pl.BlockSpec(a.shape, lambda g, t: (0,) * a.ndim)
    return pl.pallas_call(
        _ab_out_kernel,
        grid=(N_GROUPS, nt),
        in_specs=[
            pl.BlockSpec((None, HALO, CONV_CH),
                         lambda g, t: (g, jnp.maximum(t * per - 1, 0), 0)),
            tok(CONV_CH),
            pl.BlockSpec((None, HALO, CONV_CH),
                         lambda g, t: (g, jnp.minimum((t + 1) * per, nh - 1), 0)),
            tok(M_WIDTH), tok(M_WIDTH), tok(M_WIDTH), tok(D_MODEL),
            pl.BlockSpec((None, 6, D_MODEL), lambda g, t: (g, 0, 0)),
            const(conv_w), const(conv_b), const(ln_g), const(ln_b), const(mnorm_g),
            const(w_u), const(w_m),
        ],
        out_specs=tok(D_MODEL),
        out_shape=jax.ShapeDtypeStruct((N_GROUPS, GROUP_TOKENS, D_MODEL), F32),
        scratch_shapes=[pltpu.VMEM((T + 2 * HALO, CONV_CH), F32),
                        pltpu.VMEM((T, CONV_CH), F32)],
        compiler_params=_ARB2,
        name="ab_out_proj",
    )(u, u, u, hf, hb, o, x, mod, conv_w, conv_b, ln_g, ln_b, mnorm_g, w_u, w_m)


def _ffn_kernel(x_ref, mod_ref, g_ref, win_ref, wout_ref, fg_ref, out_ref, *, final_norm):
    x = x_ref[...]
    hb = _modulated(x, g_ref[...], mod_ref[3:4, :], mod_ref[4:5, :]).astype(BF16)
    acc = jnp.zeros(x.shape, F32)
    for c in range(D_FF // FFN_CHUNK):
        lo = c * FFN_CHUNK
        gt = _dot(hb, win_ref[:, lo:lo + FFN_CHUNK])
        up = _dot(hb, win_ref[:, D_FF + lo:D_FF + lo + FFN_CHUNK])
        act = (gt * _sigmoid(gt) * up).astype(BF16)
        acc = acc + _dot(act, wout_ref[lo:lo + FFN_CHUNK, :])
    y = x + mod_ref[5:6, :] * acc
    if final_norm:
        y = _rms(y) * fg_ref[...]
    out_ref[...] = y


def _ffn(x, mod, gain, w_in, w_out, final_gain, final_norm):
    tm = TOKEN_TILE
    tok = pl.BlockSpec((None, tm, D_MODEL), lambda g, t: (g, t, 0))
    const = lambda a: pl.BlockSpec(a.shape, lambda g, t: (0,) * a.ndim)
    return pl.pallas_call(
        functools.partial(_ffn_kernel, final_norm=final_norm),
        grid=(N_GROUPS, GROUP_TOKENS // tm),
        in_specs=[tok, pl.BlockSpec((None, 6, D_MODEL), lambda g, t: (g, 0, 0)),
                  const(gain), const(w_in), const(w_out), const(final_gain)],
        out_specs=tok,
        out_shape=jax.ShapeDtypeStruct(x.shape, F32),
        compiler_params=_ARB2,
        name="ffn_final" if final_norm else "ffn",
    )(x, mod, gain, w_in, w_out, final_gain)


def _rope(x, cos, sin_lo, sin_hi):
    quarter = HEAD_DIM // 4
    return (x * cos + pltpu.roll(x, HEAD_DIM - quarter, 1) * sin_lo
            + pltpu.roll(x, quarter, 1) * sin_hi)


def _c_in_kernel(x_ref, mod_ref, g_ref, w_ref, qg_ref, kg_ref, cos_ref, slo_ref, shi_ref,
                 q_ref, k_ref, v_ref, kn_ref, vn_ref):
    hb = _modulated(x_ref[...], g_ref[...], mod_ref[0:1, :], mod_ref[1:2, :]).astype(BF16)
    cos = cos_ref[...]
    slo = slo_ref[...]
    shi = shi_ref[...]
    q = _dot(hb, w_ref[:, 0:N_Q * HEAD_DIM])
    qg = qg_ref[...] * (HEAD_DIM ** -0.5)
    for h in range(N_Q):
        lanes = slice(h * HEAD_DIM, (h + 1) * HEAD_DIM)
        q_ref[:, lanes] = _rope(_rms(q[:, lanes]) * qg, cos, slo, shi).astype(BF16)
    kv = _dot(hb, w_ref[:, N_Q * HEAD_DIM:(N_Q + 2 * N_KV) * HEAD_DIM])
    kg = kg_ref[...]
    for h in range(N_KV):
        lanes = slice(h * HEAD_DIM, (h + 1) * HEAD_DIM)
        kn = _rms(kv[:, lanes]) * kg
        kn_ref[:, lanes] = kn
        k_ref[:, lanes] = _rope(kn, cos, slo, shi).astype(BF16)
    v = kv[:, N_KV * HEAD_DIM:]
    vn_ref[...] = v
    v_ref[...] = v.astype(BF16)


def _c_in(x, mod, gain, w, qg, kg, cos, sin_lo, sin_hi):
    tm = TOKEN_TILE
    kvw = N_KV * HEAD_DIM
    tok = lambda width: pl.BlockSpec((None, tm, width), lambda g, t: (g, t, 0))
    const = lambda a: pl.BlockSpec(a.shape, lambda g, t: (0,) * a.ndim)
    table = pl.BlockSpec((None, tm, HEAD_DIM), lambda g, t: (jnp.minimum(g, 1), t, 0))
    sds = lambda width, dt: jax.ShapeDtypeStruct((N_GROUPS, GROUP_TOKENS, width), dt)
    return pl.pallas_call(
        _c_in_kernel,
        grid=(N_GROUPS, GROUP_TOKENS // tm),
        in_specs=[tok(D_MODEL), pl.BlockSpec((None, 6, D_MODEL), lambda g, t: (g, 0, 0)),
                  const(gain), const(w), const(qg), const(kg), table, table, table],
        out_specs=[tok(N_Q * HEAD_DIM), tok(kvw), tok(kvw), tok(kvw), tok(kvw)],
        out_shape=[sds(N_Q * HEAD_DIM, BF16), sds(kvw, BF16), sds(kvw, BF16),
                   sds(kvw, F32), sds(kvw, F32)],
        compiler_params=_ARB2,
        name="attn_in_proj",
    )(x, mod, gain, w, qg, kg, cos, sin_lo, sin_hi)


def _softmax_pv(q, parts):
    scores = [_dot_nt(q, k) for k, _ in parts]
    m = scores[0].max(axis=-1, keepdims=True)
    for s in scores[1:]:
        m = jnp.maximum(m, s.max(axis=-1, keepdims=True))
    num = None
    den = None
    for s, (_, v) in zip(scores, parts):
        p = jnp.exp(s - m)
        l = jnp.sum(p, axis=-1, keepdims=True)
        o = _dot(p.astype(BF16), v)
        num = o if num is None else num + o
        den = l if den is None else den + l
    return num / den


def _attn_kernel(q_ref, k_ref, v_ref, kc_ref, vc_ref, o_ref):
    g = pl.program_id(0)
    t = pl.program_id(2)
    T = q_ref.shape[0]

    @pl.when(g == 0)
    def _():
        rows = pl.ds(pl.multiple_of(t * T, T), T)
        k = k_ref[rows, :]
        v = v_ref[rows, :]
        for i in range(GROUP):
            lanes = slice(i * HEAD_DIM, (i + 1) * HEAD_DIM)
            o_ref[:, lanes] = _softmax_pv(q_ref[:, lanes], [(k, v)]).astype(BF16)

    @pl.when(g > 0)
    def _():
        parts = [(k_ref[...], v_ref[...]), (kc_ref[...], vc_ref[...])]
        for i in range(GROUP):
            lanes = slice(i * HEAD_DIM, (i + 1) * HEAD_DIM)
            o_ref[:, lanes] = _softmax_pv(q_ref[:, lanes], parts).astype(BF16)


def _attention(q, k, v, cache_k, cache_v):
    T = SEQ_TILE
    assert T == CTX_SEQ
    gw = GROUP * HEAD_DIM
    qspec = pl.BlockSpec((None, T, gw), lambda g, h, t: (g, t, h))
    kspec = pl.BlockSpec((None, GROUP_TOKENS, HEAD_DIM), lambda g, h, t: (g, 0, h))
    cspec = pl.BlockSpec((None, PAST_LEN, HEAD_DIM),
                         lambda g, h, t: (jnp.maximum(g - 1, 0), 0, h))
    return pl.pallas_call(
        _attn_kernel,
        grid=(N_GROUPS, N_KV, GROUP_TOKENS // T),
        in_specs=[qspec, kspec, kspec, cspec, cspec],
        out_specs=qspec,
        out_shape=jax.ShapeDtypeStruct(q.shape, BF16),
        compiler_params=_ARB3,
        name="gqa_attention",
    )(q, k, v, cache_k, cache_v)


def _c_out_kernel(o_ref, x_ref, mod_ref, w_ref, out_ref):
    out_ref[...] = x_ref[...] + mod_ref[2:3, :] * _dot(o_ref[...], w_ref[...])


def _c_out(o, x, mod, w):
    tm = TOKEN_TILE
    tok = pl.BlockSpec((None, tm, D_MODEL), lambda g, t: (g, t, 0))
    return pl.pallas_call(
        _c_out_kernel,
        grid=(N_GROUPS, GROUP_TOKENS // tm),
        in_specs=[tok, tok, pl.BlockSpec((None, 6, D_MODEL), lambda g, t: (g, 0, 0)),
                  pl.BlockSpec(w.shape, lambda g, t: (0, 0))],
        out_specs=tok,
        out_shape=jax.ShapeDtypeStruct(x.shape, F32),
        compiler_params=_ARB2,
        name="attn_out_proj",
    )(o, x, mod, w)


def _rope_tables():
    pos = jnp.arange(GROUP_TOKENS)
    n_freq = HEAD_DIM // 4
    inv = 1.0 / (ROPE_THETA ** (jnp.arange(n_freq, dtype=F32) / n_freq))
    ang_r = (pos // GRID_W).astype(F32)[:, None] * inv
    ang_c = (pos % GRID_W).astype(F32)[:, None] * inv
    zero = jnp.zeros_like(ang_r)
    cos = jnp.concatenate([jnp.cos(ang_r)] * 2 + [jnp.cos(ang_c)] * 2, axis=-1)
    sin_lo = jnp.concatenate([-jnp.sin(ang_r), zero, -jnp.sin(ang_c), zero], axis=-1)
    sin_hi = jnp.concatenate([zero, jnp.sin(ang_r), zero, jnp.sin(ang_c)], axis=-1)
    ident = lambda a, fill: jnp.stack([jnp.full_like(a, fill), a])
    return ident(cos, 1.0), ident(sin_lo, 0.0), ident(sin_hi, 0.0)


def kernel(x_prompt, x_sample, c, state_mlstm_C, state_mlstm_n, state_mlstm_m, cache_k, cache_v,
           c_ctx, w_mod, b_mod, norm1_g, norm2_g, w_in_ab, b_gate_ab, conv_w, conv_b, conv_ln_g,
           conv_ln_b, mlstm_norm_g, w_out_ab, w_in_c, q_norm_g, k_norm_g, w_out_c, w_ffn_in,
           w_ffn_out, final_norm_g):
    n_lat = x_sample.shape[0]
    n_ctx = x_prompt.shape[0]
    x = jnp.concatenate([x_prompt.reshape(1, GROUP_TOKENS, D_MODEL), x_sample], axis=0)

    cond8 = jnp.concatenate([c_ctx[None, :], c, jnp.zeros((8 - 1 - n_lat, D_MODEL), F32)], axis=0)
    mods = _adaln_all(cond8, w_mod, b_mod)[:, :N_GROUPS].reshape(2, N_GROUPS, 6, D_MODEL)
    row = lambda a: a.reshape(1, -1)

    main_cols = 2 * CONV_CH + 4 * M_WIDTH
    w_ab = w_in_ab[0]
    w_main = w_ab[:, :main_cols].astype(BF16)
    w_gate = jnp.pad(w_ab[:, main_cols:], ((0, 0), (0, 128 - 4 * M_HEADS))).astype(BF16)
    w_gate_t = w_ab[:, main_cols:].T.astype(BF16)
    b_gate_row = jnp.pad(b_gate_ab[0], (0, 128 - 4 * M_HEADS)).reshape(1, 128)
    b_gate_col = b_gate_ab[0].reshape(4 * M_HEADS, 1)
    u, q, k, v, o, gc, gr = _ab_in(x, mods[0], row(norm1_g[0]), w_main, w_gate, w_gate_t,
                                   b_gate_row, b_gate_col)

    n_streams = 2 * M_HEADS
    c0 = state_mlstm_C[:, 0].reshape(n_lat, n_streams, M_DK, M_DK)
    n0 = state_mlstm_n[:, 0].reshape(n_lat, n_streams, M_DK)
    m0 = jnp.broadcast_to(state_mlstm_m[:, 0].reshape(n_lat, n_streams, 1), (n_lat, n_streams, 128))
    hf, hb, new_c, new_n, new_m = _mlstm(q, k, v, gc, gr, c0, n0, m0)

    conv_w_pad = jnp.pad(conv_w[0], ((0, 32 - CONV_WIDTH), (0, 0)))
    w_o = w_out_ab[0].astype(BF16)
    x = _ab_out(u, hf, hb, o, x, mods[0], conv_w_pad, row(conv_b[0]), row(conv_ln_g[0]),
                row(conv_ln_b[0]), row(mlstm_norm_g[0]), w_o[:CONV_CH], w_o[CONV_CH:])
    fg = row(final_norm_g)
    x = _ffn(x, mods[0], row(norm2_g[0]), w_ffn_in[0].astype(BF16), w_ffn_out[0].astype(BF16),
             fg, False)

    cos, sin_lo, sin_hi = _rope_tables()
    qa, ka, va, k_norm, v_raw = _c_in(x, mods[1], row(norm1_g[1]), w_in_c[0].astype(BF16),
                                      row(q_norm_g[0]), row(k_norm_g[0]), cos, sin_lo, sin_hi)
    kvw = N_KV * HEAD_DIM
    ck = cache_k[:, 0].reshape(n_lat, PAST_LEN, kvw).astype(BF16)
    cv = cache_v[:, 0].reshape(n_lat, PAST_LEN, kvw).astype(BF16)
    oa = _attention(qa, ka, va, ck, cv)
    x = _c_out(oa, x, mods[1], w_out_c[0].astype(BF16))
    y = _ffn(x, mods[1], row(norm2_g[1]), w_ffn_in[1].astype(BF16), w_ffn_out[1].astype(BF16),
             fg, True)

    y_prompt = y[0].reshape(n_ctx, CTX_SEQ, D_MODEL)
    y_sample = y[1:]
    new_state_c = new_c.reshape(n_ctx, 1, 2, M_HEADS, M_DK, M_DK)
    new_state_n = new_n.reshape(n_ctx, 1, 2, M_HEADS, M_DK)
    new_state_m = new_m[:, :, 0].reshape(n_ctx, 1, 2, M_HEADS)
    new_k = k_norm[0].reshape(n_ctx, 1, CTX_SEQ, N_KV, HEAD_DIM)
    new_v = v_raw[0].reshape(n_ctx, 1, CTX_SEQ, N_KV, HEAD_DIM)
    return (y_prompt, y_sample, new_state_c, new_state_n, new_state_m, new_k, new_v)
```

```python
import functools

import jax
import jax.numpy as jnp
from jax import lax
from jax.experimental import pallas as pl
from jax.experimental.pallas import tpu as pltpu

F32 = jnp.float32
BF16 = jnp.bfloat16

D_MODEL = 1024
N_GROUPS = 5
GROUP_TOKENS = 4096
CTX_SEQ = 256
CONV_CH = 512
CONV_WIDTH = 31
M_HEADS = 4
M_DK = 128
M_WIDTH = 512
N_STREAMS = 2 * M_HEADS
STATE_ROWS = M_DK + 16
HEAD_DIM = 128
N_Q = 8
N_KV = 2
GROUP = N_Q // N_KV
PAST_LEN = 256
GRID_W = 64
ROPE_THETA = 10000.0
D_FF = 2816
EPS = 1e-6
LOG2_E = 1.4426950408889634

CHUNK = 256
TOKEN_TILE = 512
SEQ_TILE = 256
ATTN_TILE = 512
FFN_CUTS = (0, 1536, D_FF)
HALO = 16
SUBLANES = 8
VMEM_LIMIT = 56 * 1024 * 1024

_ARB2 = pltpu.CompilerParams(dimension_semantics=("arbitrary", "arbitrary"),
                             vmem_limit_bytes=VMEM_LIMIT)
_ARB3 = pltpu.CompilerParams(dimension_semantics=("arbitrary", "arbitrary", "arbitrary"),
                             vmem_limit_bytes=VMEM_LIMIT)


def _rms(x):
    return x * lax.rsqrt(jnp.mean(x * x, axis=-1, keepdims=True) + EPS)


def _modulated(x, gain, shift, scale):
    return (_rms(x) * gain) * (1.0 + scale) + shift


def _sigmoid(x):
    return 1.0 / (1.0 + jnp.exp(-x))


def _log_sigmoid(x):
    return jnp.minimum(x, 0.0) - jnp.log1p(jnp.exp(-jnp.abs(x)))


def _dot(a, b):
    return jnp.dot(a, b, preferred_element_type=F32)


def _dot_nt(a, b):
    return lax.dot_general(a, b, (((1,), (1,)), ((), ())), preferred_element_type=F32)


def _split3(x):
    hi = x.astype(BF16)
    r1 = x - hi.astype(F32)
    mid = r1.astype(BF16)
    lo = (r1 - mid.astype(F32)).astype(BF16)
    return hi, mid, lo


def _ctx_spec(tile, nt, width):
    return pl.BlockSpec((None, tile, width), lambda g, t: (0, jnp.where(g == 0, t, nt - 1), 0))


def _lat_spec(tile, width):
    return pl.BlockSpec((None, tile, width),
                        lambda g, t: (jnp.maximum(g - 1, 0), jnp.where(g > 0, t, 0), 0))


def _mod_kernel(cond_ref, w_ref, b_ref, o_ref):
    c = cond_ref[...]
    s = (c * _sigmoid(c)).astype(BF16)
    o_ref[...] = _dot(s, w_ref[...].astype(BF16)) + b_ref[...]


def _adaln_all(cond8, w_mod, b_mod):
    depth, d, n = w_mod.shape
    tn = 1024
    return pl.pallas_call(
        _mod_kernel,
        grid=(depth, n // tn),
        in_specs=[
            pl.BlockSpec((8, d), lambda l, j: (0, 0)),
            pl.BlockSpec((None, d, tn), lambda l, j: (l, 0, j)),
            pl.BlockSpec((None, 1, tn), lambda l, j: (l, 0, j)),
        ],
        out_specs=pl.BlockSpec((None, 8, tn), lambda l, j: (l, 0, j)),
        out_shape=jax.ShapeDtypeStruct((depth, 8, n), F32),
        compiler_params=_ARB2,
        name="adaln_mod",
    )(cond8, w_mod, b_mod.reshape(depth, 1, n))


def _ab_in_kernel(xp_ref, xs_ref, mod_ref, g_ref, w_ref, wqt_ref, wvt_ref,
                  wgi_ref, wgf_ref, wgft_ref, bgi_ref, bgf_ref, bgft_ref,
                  u_ref, qt_ref, k_ref, vt_ref, o_ref, gi_ref, gf_ref, gft_ref):
    x = jnp.where(pl.program_id(0) == 0, xp_ref[...], xs_ref[...])
    h = _modulated(x, g_ref[...], mod_ref[0:1, :], mod_ref[1:2, :])
    hb = h.astype(BF16)

    def proj(c):
        return _dot(hb, w_ref[:, c * CONV_CH:(c + 1) * CONV_CH])

    u_ref[...] = proj(0) * _sigmoid(proj(1))
    k_ref[...] = proj(3) * (M_DK ** -0.5)
    o_ref[...] = proj(5)
    qt_ref[...] = _dot_nt(wqt_ref[...], hb).astype(BF16)
    vt_ref[...] = _dot_nt(wvt_ref[...], hb).astype(BF16)
    lane = lax.broadcasted_iota(jnp.int32, (x.shape[0], 128), 1)
    gi_ref[...] = _dot(hb, wgi_ref[...]) + bgi_ref[...]
    gf = _log_sigmoid(_dot(hb, wgf_ref[...]) + bgf_ref[...])
    gf_ref[...] = jnp.where(lane < N_STREAMS, gf, 0.0)
    gft_ref[...] = _log_sigmoid(_dot_nt(wgft_ref[...], hb) + bgft_ref[...])


def _ab_in(xp, xs, mod, gain, w_main, wqt, wvt, wgi, wgf, wgft, bgi, bgf, bgft):
    tm = TOKEN_TILE
    nt = GROUP_TOKENS // tm
    tok = lambda width: pl.BlockSpec((None, tm, width), lambda g, t: (g, t, 0))
    feat = lambda rows: pl.BlockSpec((None, rows, tm), lambda g, t: (g, 0, t))
    const = lambda a: pl.BlockSpec(a.shape, lambda g, t: (0,) * a.ndim)
    tok_sds = lambda width: jax.ShapeDtypeStruct((N_GROUPS, GROUP_TOKENS, width), F32)
    feat_sds = lambda rows, dt: jax.ShapeDtypeStruct((N_GROUPS, rows, GROUP_TOKENS), dt)
    consts = (gain, w_main, wqt, wvt, wgi, wgf, wgft, bgi, bgf, bgft)
    return pl.pallas_call(
        _ab_in_kernel,
        grid=(N_GROUPS, nt),
        in_specs=[_ctx_spec(tm, nt, D_MODEL), _lat_spec(tm, D_MODEL),
                  pl.BlockSpec((None, 6, D_MODEL), lambda g, t: (g, 0, 0))]
                 + [const(a) for a in consts],
        out_specs=[tok(CONV_CH), feat(M_WIDTH), tok(M_WIDTH), feat(M_WIDTH), tok(M_WIDTH),
                   tok(128), tok(128), feat(N_STREAMS)],
        out_shape=[tok_sds(CONV_CH), feat_sds(M_WIDTH, BF16), tok_sds(M_WIDTH),
                   feat_sds(M_WIDTH, BF16), tok_sds(M_WIDTH),
                   tok_sds(128), tok_sds(128), feat_sds(N_STREAMS, F32)],
        compiler_params=_ARB2,
        name="ab_in_proj",
    )(xp, xs, mod, *consts)


def _mlstm_direction(qt_ref, k_ref, vt_ref, gi_ref, gf_ref, gft_ref, h_ref, cn_s, m_s, d):
    L = k_ref.shape[0]
    r = lax.broadcasted_iota(jnp.int32, (L, L), 0)
    c = lax.broadcasted_iota(jnp.int32, (L, L), 1)
    le = r <= c
    ge = r >= c
    gf = gf_ref[...]
    gft = gft_ref[...]
    tri_col = ge.astype(BF16)
    tri_row = le.astype(BF16)
    p_col = sum(_dot(tri_col, part) for part in _split3(gf))
    p_row = sum(_dot(part, tri_row) for part in _split3(gft))
    tot_col = p_col[L - 1:L, :]
    if d == 0:
        b_col, b_row, mask = p_col, p_row, le
    else:
        b_col = tot_col - p_col + gf
        b_row = p_row[:, L - 1:L] - p_row + gft
        mask = ge
    c_all = gi_ref[...] - b_col
    m_prev_all = m_s[...]
    m_fin = jnp.maximum(m_prev_all, jnp.max(c_all, axis=0, keepdims=True))
    w_all = jnp.exp(c_all - m_fin)
    a_all = jnp.exp(m_prev_all - m_fin)
    lane = lax.broadcasted_iota(jnp.int32, (1, 128), 1)
    mine = (lane >= d * M_HEADS) & (lane < (d + 1) * M_HEADS)
    m_s[...] = jnp.where(mine, tot_col + m_fin, m_prev_all)
    ones_row = jnp.where(lax.broadcasted_iota(jnp.int32, (STATE_ROWS - M_DK, L), 0) == 0,
                         1.0, 0.0).astype(BF16)
    for h in range(M_HEADS):
        s_idx = d * M_HEADS + h
        sl = slice(h * M_DK, (h + 1) * M_DK)
        c_col = c_all[:, s_idx:s_idx + 1]
        m_prev = m_prev_all[:, s_idx:s_idx + 1]
        cm = jnp.where(mask, c_col, -jnp.inf)
        m_row = jnp.maximum(jnp.max(cm, axis=0, keepdims=True), m_prev)
        dmat = jnp.exp(cm - m_row)
        k = k_ref[:, sl]
        cn = cn_s[s_idx]
        lhs = jnp.concatenate([k.astype(BF16), cn.astype(BF16)], axis=0)
        r1 = _dot(lhs, qt_ref[sl, :])
        st = (r1[0:L] * dmat).astype(BF16)
        vt_aug = jnp.concatenate([vt_ref[sl, :], ones_row], axis=0)
        w_inter = jnp.exp(m_prev - m_row)
        tot = _dot(vt_aug, st) + w_inter * r1[L:L + STATE_ROWS]
        den = tot[M_DK:M_DK + 1, :]
        m_t = b_row[s_idx:s_idx + 1, :] + m_row
        ht = tot[0:M_DK, :] / jnp.maximum(jnp.abs(den), jnp.exp(-m_t))
        h_ref[:, sl] = ht.T
        kw = (k * w_all[:, s_idx:s_idx + 1]).astype(BF16)
        cn_s[s_idx] = a_all[:, s_idx:s_idx + 1] * cn + _dot(vt_aug, kw)


def _mlstm_kernel(qtf_ref, kf_ref, vtf_ref, gif_ref, gff_ref, gftf_ref,
                  qtb_ref, kb_ref, vtb_ref, gib_ref, gfb_ref, gftb_ref,
                  c0_ref, n0_ref, m0_ref,
                  hf_ref, hb_ref, co_ref, no_ref, mo_ref,
                  cn_s, m_s):
    g = pl.program_id(0)
    c = pl.program_id(1)

    @pl.when(g == 0)
    def _():
        cn_s[...] = jnp.zeros_like(cn_s)
        m_s[...] = jnp.zeros_like(m_s)

    @pl.when((g > 0) & (c == 0))
    def _():
        cn_s[...] = jnp.zeros_like(cn_s)
        for s in range(N_STREAMS):
            cn_s[s, 0:M_DK, :] = c0_ref[s].T
            cn_s[s, M_DK:M_DK + 1, :] = n0_ref[s:s + 1, :]
        m_s[...] = m0_ref[...]

    _mlstm_direction(qtf_ref, kf_ref, vtf_ref, gif_ref, gff_ref, gftf_ref, hf_ref, cn_s, m_s, 0)
    _mlstm_direction(qtb_ref, kb_ref, vtb_ref, gib_ref, gfb_ref, gftb_ref, hb_ref, cn_s, m_s, 1)

    @pl.when(g == 0)
    def _():
        for s in range(N_STREAMS):
            co_ref[s] = cn_s[s, 0:M_DK, :].T
            no_ref[s:s + 1, :] = cn_s[s, M_DK:M_DK + 1, :]
        mo_ref[...] = m_s[...]


def _mlstm(qt, k, vt, gi, gf, gft, c0, n0, m0):
    L = CHUNK
    nc = GROUP_TOKENS // L
    n_ctx = GROUP_TOKENS // CTX_SEQ
    assert CTX_SEQ == L and n_ctx == nc
    fwd = lambda g, c: c
    bwd = lambda g, c: jnp.where(g == 0, c, nc - 1 - c)
    tok = lambda width, pos: pl.BlockSpec((None, L, width), lambda g, c: (g, pos(g, c), 0))
    feat = lambda rows, pos: pl.BlockSpec((None, rows, L), lambda g, c: (g, 0, pos(g, c)))
    lat = lambda g, c: jnp.maximum(g - 1, 0)
    ctx = lambda g, c: jnp.where(g == 0, c, n_ctx - 1)
    per_dir = lambda pos: [feat(M_WIDTH, pos), tok(M_WIDTH, pos), feat(M_WIDTH, pos),
                           tok(128, pos), tok(128, pos), feat(N_STREAMS, pos)]
    return pl.pallas_call(
        _mlstm_kernel,
        grid=(N_GROUPS, nc),
        in_specs=per_dir(fwd) + per_dir(bwd) + [
            pl.BlockSpec((None, N_STREAMS, M_DK, M_DK), lambda g, c: (lat(g, c), 0, 0, 0)),
            pl.BlockSpec((None, N_STREAMS, M_DK), lambda g, c: (lat(g, c), 0, 0)),
            pl.BlockSpec((None, 1, 128), lambda g, c: (lat(g, c), 0, 0)),
        ],
        out_specs=[
            tok(M_WIDTH, fwd), tok(M_WIDTH, bwd),
            pl.BlockSpec((None, N_STREAMS, M_DK, M_DK), lambda g, c: (ctx(g, c), 0, 0, 0)),
            pl.BlockSpec((None, N_STREAMS, M_DK), lambda g, c: (ctx(g, c), 0, 0)),
            pl.BlockSpec((None, 1, 128), lambda g, c: (ctx(g, c), 0, 0)),
        ],
        out_shape=[
            jax.ShapeDtypeStruct((N_GROUPS, GROUP_TOKENS, M_WIDTH), F32),
            jax.ShapeDtypeStruct((N_GROUPS, GROUP_TOKENS, M_WIDTH), F32),
            jax.ShapeDtypeStruct((n_ctx, N_STREAMS, M_DK, M_DK), F32),
            jax.ShapeDtypeStruct((n_ctx, N_STREAMS, M_DK), F32),
            jax.ShapeDtypeStruct((n_ctx, 1, 128), F32),
        ],
        scratch_shapes=[
            pltpu.VMEM((N_STREAMS, STATE_ROWS, M_DK), F32),
            pltpu.VMEM((1, 128), F32),
        ],
        compiler_params=_ARB2,
        name="mlstm_scan",
    )(qt, k, vt, gi, gf, gft, qt, k, vt, gi, gf, gft, c0, n0, m0)


def _ab_out_kernel(up_ref, uc_ref, un_ref, hf_ref, hb_ref, o_ref, xp_ref, xs_ref, mod_ref,
                   cw_ref, cb_ref, lg_ref, lb_ref, mg_ref, wu_ref, wm_ref,
                   out_ref, ext_ref, conv_ref):
    g = pl.program_id(0)
    t = pl.program_id(1)
    nt = pl.num_programs(1)
    T = uc_ref.shape[0]
    prev_on = jnp.where((g > 0) & (t > 0), 1.0, 0.0).astype(F32)
    next_on = jnp.where((g > 0) & (t < nt - 1), 1.0, 0.0).astype(F32)
    ext_ref[0, 0:HALO, :] = up_ref[...] * prev_on
    ext_ref[0, HALO:HALO + T, :] = uc_ref[...]
    ext_ref[0, HALO + T:HALO + T + HALO, :] = un_ref[...] * next_on
    span = T + 2 * HALO - SUBLANES
    for s in range(1, SUBLANES):
        ext_ref[s, 0:span, :] = ext_ref[0, s:s + span, :]

    rb = 32
    off = HALO - CONV_WIDTH // 2
    for r0 in range(0, T, rb):
        acc = jnp.broadcast_to(cb_ref[...], (rb, CONV_CH))
        for j in range(CONV_WIDTH):
            phase, base = (off + j) % SUBLANES, (off + j) // SUBLANES * SUBLANES
            acc = acc + cw_ref[j:j + 1, :] * ext_ref[phase, r0 + base:r0 + base + rb, :]
        conv_ref[r0:r0 + rb, :] = acc

    y = conv_ref[...]
    mu = jnp.mean(y, axis=-1, keepdims=True)
    yc = y - mu
    var = jnp.mean(yc * yc, axis=-1, keepdims=True)
    y = yc * lax.rsqrt(var + EPS) * lg_ref[...] + lb_ref[...]
    u = y * _sigmoid(y)

    hm = hf_ref[...] + hb_ref[...]
    hm = jnp.concatenate(
        [_rms(hm[:, h * M_DK:(h + 1) * M_DK]) for h in range(M_HEADS)], axis=-1)
    hm = _sigmoid(o_ref[...]) * (hm * mg_ref[...])
    out = _dot(u.astype(BF16), wu_ref[...]) + _dot(hm.astype(BF16), wm_ref[...])
    x = jnp.where(g == 0, xp_ref[...], xs_ref[...])
    out_ref[...] = x + mod_ref[2:3, :] * out


def _ab_out(u, hf, hb, o, xp, xs, mod, conv_w, conv_b, ln_g, ln_b, mnorm_g, w_u, w_m):
    T = SEQ_TILE
    nt = GROUP_TOKENS // T
    per = T // HALO
    nh = GROUP_TOKENS // HALO
    tok = lambda width: pl.BlockSpec((None, T, width), lambda g, t: (g, t, 0))
    const = lambda a: pl.BlockSpec(a.shape, lambda g, t: (0,) * a.ndim)
    consts = (conv_w, conv_b, ln_g, ln_b, mnorm_g, w_u, w_m)
    return pl.pallas_call(
        _ab_out_kernel,
        grid=(N_GROUPS, nt),
        in_specs=[
            pl.BlockSpec((None, HALO, CONV_CH),
                         lambda g, t: (g, jnp.maximum(t * per - 1, 0), 0)),
            tok(CONV_CH),
            pl.BlockSpec((None, HALO, CONV_CH),
                         lambda g, t: (g, jnp.minimum((t + 1) * per, nh - 1), 0)),
            tok(M_WIDTH), tok(M_WIDTH), tok(M_WIDTH),
            _ctx_spec(T, nt, D_MODEL), _lat_spec(T, D_MODEL),
            pl.BlockSpec((None, 6, D_MODEL), lambda g, t: (g, 0, 0)),
        ] + [const(a) for a in consts],
        out_specs=tok(D_MODEL),
        out_shape=jax.ShapeDtypeStruct((N_GROUPS, GROUP_TOKENS, D_MODEL), F32),
        scratch_shapes=[pltpu.VMEM((SUBLANES, T + 2 * HALO, CONV_CH), F32),
                        pltpu.VMEM((T, CONV_CH), F32)],
        compiler_params=_ARB2,
        name="ab_out_proj",
    )(u, u, u, hf, hb, o, xp, xs, mod, *consts)


def _ffn_body(x_ref, mod_ref, g_ref, win_ref, wout_ref):
    x = x_ref[...]
    hb = _modulated(x, g_ref[...], mod_ref[3:4, :], mod_ref[4:5, :]).astype(BF16)
    acc = jnp.zeros(x.shape, F32)
    for lo, hi in zip(FFN_CUTS[:-1], FFN_CUTS[1:]):
        gt = _dot(hb, win_ref[:, lo:hi])
        up = _dot(hb, win_ref[:, D_FF + lo:D_FF + hi])
        act = (gt * _sigmoid(gt) * up).astype(BF16)
        acc = acc + _dot(act, wout_ref[lo:hi, :])
    return x + mod_ref[5:6, :] * acc


def _ffn_kernel(x_ref, mod_ref, g_ref, win_ref, wout_ref, out_ref):
    out_ref[...] = _ffn_body(x_ref, mod_ref, g_ref, win_ref, wout_ref)


def _ffn_final_kernel(x_ref, mod_ref, g_ref, win_ref, wout_ref, fg_ref, yp_ref, ys_ref):
    y = _rms(_ffn_body(x_ref, mod_ref, g_ref, win_ref, wout_ref)) * fg_ref[...]
    g = pl.program_id(0)

    @pl.when(g == 0)
    def _():
        yp_ref[...] = y

    @pl.when(g > 0)
    def _():
        ys_ref[...] = y


def _ffn(x, mod, gain, w_in, w_out, final_gain=None):
    tm = TOKEN_TILE
    nt = GROUP_TOKENS // tm
    tok = pl.BlockSpec((None, tm, D_MODEL), lambda g, t: (g, t, 0))
    const = lambda a: pl.BlockSpec(a.shape, lambda g, t: (0,) * a.ndim)
    in_specs = [tok, pl.BlockSpec((None, 6, D_MODEL), lambda g, t: (g, 0, 0)),
                const(gain), const(w_in), const(w_out)]
    if final_gain is None:
        return pl.pallas_call(
            _ffn_kernel,
            grid=(N_GROUPS, nt),
            in_specs=in_specs,
            out_specs=tok,
            out_shape=jax.ShapeDtypeStruct(x.shape, F32),
            compiler_params=_ARB2,
            name="ffn",
        )(x, mod, gain, w_in, w_out)
    return pl.pallas_call(
        _ffn_final_kernel,
        grid=(N_GROUPS, nt),
        in_specs=in_specs + [const(final_gain)],
        out_specs=[_ctx_spec(tm, nt, D_MODEL), _lat_spec(tm, D_MODEL)],
        out_shape=[jax.ShapeDtypeStruct((1, GROUP_TOKENS, D_MODEL), F32),
                   jax.ShapeDtypeStruct((N_GROUPS - 1, GROUP_TOKENS, D_MODEL), F32)],
        compiler_params=_ARB2,
        name="ffn_final",
    )(x, mod, gain, w_in, w_out, final_gain)


def _rope(x, cos, sin_lo, sin_hi):
    quarter = HEAD_DIM // 4
    return (x * cos + pltpu.roll(x, HEAD_DIM - quarter, 1) * sin_lo
            + pltpu.roll(x, quarter, 1) * sin_hi)


def _c_in_kernel(x_ref, mod_ref, g_ref, w_ref, qg_ref, kg_ref, cos_ref, slo_ref, shi_ref,
                 q_ref, k_ref, v_ref, kn_ref, vn_ref):
    hb = _modulated(x_ref[...], g_ref[...], mod_ref[0:1, :], mod_ref[1:2, :]).astype(BF16)
    cos = cos_ref[...]
    slo = slo_ref[...]
    shi = shi_ref[...]
    q = _dot(hb, w_ref[:, 0:N_Q * HEAD_DIM])
    qg = qg_ref[...] * (HEAD_DIM ** -0.5 * LOG2_E)
    for h in range(N_Q):
        lanes = slice(h * HEAD_DIM, (h + 1) * HEAD_DIM)
        q_ref[:, lanes] = _rope(_rms(q[:, lanes]) * qg, cos, slo, shi).astype(BF16)
    kv = _dot(hb, w_ref[:, N_Q * HEAD_DIM:(N_Q + 2 * N_KV) * HEAD_DIM])
    kg = kg_ref[...]
    for h in range(N_KV):
        lanes = slice(h * HEAD_DIM, (h + 1) * HEAD_DIM)
        kn = _rms(kv[:, lanes]) * kg
        kn_ref[:, lanes] = kn
        k_ref[:, lanes] = _rope(kn, cos, slo, shi).astype(BF16)
    v = kv[:, N_KV * HEAD_DIM:]
    vn_ref[...] = v
    v_ref[...] = v.astype(BF16)


def _c_in(x, mod, gain, w, qg, kg, cos, sin_lo, sin_hi):
    tm = TOKEN_TILE
    kvw = N_KV * HEAD_DIM
    tok = lambda width: pl.BlockSpec((None, tm, width), lambda g, t: (g, t, 0))
    const = lambda a: pl.BlockSpec(a.shape, lambda g, t: (0,) * a.ndim)
    table = pl.BlockSpec((None, tm, HEAD_DIM), lambda g, t: (jnp.minimum(g, 1), t, 0))
    sds = lambda width, dt: jax.ShapeDtypeStruct((N_GROUPS, GROUP_TOKENS, width), dt)
    return pl.pallas_call(
        _c_in_kernel,
        grid=(N_GROUPS, GROUP_TOKENS // tm),
        in_specs=[tok(D_MODEL), pl.BlockSpec((None, 6, D_MODEL), lambda g, t: (g, 0, 0)),
                  const(gain), const(w), const(qg), const(kg), table, table, table],
        out_specs=[tok(N_Q * HEAD_DIM), tok(kvw), tok(kvw), tok(kvw), tok(kvw)],
        out_shape=[sds(N_Q * HEAD_DIM, BF16), sds(kvw, BF16), sds(kvw, BF16),
                   sds(kvw, F32), sds(kvw, F32)],
        compiler_params=_ARB2,
        name="attn_in_proj",
    )(x, mod, gain, w, qg, kg, cos, sin_lo, sin_hi)


def _group_softmax_pv(q_ref, o_ref, rows, k, v_aug):
    for i in range(GROUP):
        lanes = slice(i * HEAD_DIM, (i + 1) * HEAD_DIM)
        s = _dot_nt(q_ref[rows, lanes], k)
        p = jnp.exp2(s - jnp.max(s, axis=-1, keepdims=True)).astype(BF16)
        o = _dot(p, v_aug)
        o_ref[rows, lanes] = (o[:, :HEAD_DIM] / o[:, HEAD_DIM:HEAD_DIM + 1]).astype(BF16)


def _ones_column(rows):
    lane = lax.broadcasted_iota(jnp.int32, (rows, HEAD_DIM), 1)
    return jnp.where(lane == 0, 1.0, 0.0).astype(BF16)


def _attn_kernel(q_ref, k_ref, v_ref, kc_ref, vc_ref, o_ref, kall_ref, vall_ref):
    g = pl.program_id(0)
    t = pl.program_id(2)
    T = q_ref.shape[0]
    S = k_ref.shape[0]

    @pl.when(g == 0)
    def _():
        for j in range(T // CTX_SEQ):
            keys = pl.ds(pl.multiple_of(t * T + j * CTX_SEQ, CTX_SEQ), CTX_SEQ)
            v_aug = jnp.concatenate([v_ref[keys, :], _ones_column(CTX_SEQ)], axis=-1)
            _group_softmax_pv(q_ref, o_ref, slice(j * CTX_SEQ, (j + 1) * CTX_SEQ),
                              k_ref[keys, :], v_aug)

    @pl.when((g > 0) & (t == 0))
    def _():
        kall_ref[0:S, :] = k_ref[...]
        kall_ref[S:, :] = kc_ref[...]
        vall_ref[0:S, 0:HEAD_DIM] = v_ref[...]
        vall_ref[S:, 0:HEAD_DIM] = vc_ref[...]
        vall_ref[:, HEAD_DIM:] = _ones_column(vall_ref.shape[0])

    @pl.when(g > 0)
    def _():
        _group_softmax_pv(q_ref, o_ref, slice(0, T), kall_ref[...], vall_ref[...])


def _attention(q, k, v, cache_k, cache_v):
    T = ATTN_TILE
    assert T % CTX_SEQ == 0
    gw = GROUP * HEAD_DIM
    qspec = pl.BlockSpec((None, T, gw), lambda g, h, t: (g, t, h))
    kspec = pl.BlockSpec((None, GROUP_TOKENS, HEAD_DIM), lambda g, h, t: (g, 0, h))
    cspec = pl.BlockSpec((None, PAST_LEN, HEAD_DIM),
                         lambda g, h, t: (jnp.maximum(g - 1, 0), 0, h))
    n_keys = GROUP_TOKENS + PAST_LEN
    return pl.pallas_call(
        _attn_kernel,
        grid=(N_GROUPS, N_KV, GROUP_TOKENS // T),
        in_specs=[qspec, kspec, kspec, cspec, cspec],
        out_specs=qspec,
        out_shape=jax.ShapeDtypeStruct(q.shape, BF16),
        scratch_shapes=[pltpu.VMEM((n_keys, HEAD_DIM), BF16),
                        pltpu.VMEM((n_keys, 2 * HEAD_DIM), BF16)],
        compiler_params=_ARB3,
        name="gqa_attention",
    )(q, k, v, cache_k, cache_v)


def _c_out_kernel(o_ref, x_ref, mod_ref, w_ref, out_ref):
    out_ref[...] = x_ref[...] + mod_ref[2:3, :] * _dot(o_ref[...], w_ref[...])


def _c_out(o, x, mod, w):
    tm = TOKEN_TILE
    tok = pl.BlockSpec((None, tm, D_MODEL), lambda g, t: (g, t, 0))
    return pl.pallas_call(
        _c_out_kernel,
        grid=(N_GROUPS, GROUP_TOKENS // tm),
        in_specs=[tok, tok, pl.BlockSpec((None, 6, D_MODEL), lambda g, t: (g, 0, 0)),
                  pl.BlockSpec(w.shape, lambda g, t: (0, 0))],
        out_specs=tok,
        out_shape=jax.ShapeDtypeStruct(x.shape, F32),
        compiler_params=_ARB2,
        name="attn_out_proj",
    )(o, x, mod, w)


def _rope_tables():
    pos = jnp.arange(GROUP_TOKENS)
    n_freq = HEAD_DIM // 4
    inv = 1.0 / (ROPE_THETA ** (jnp.arange(n_freq, dtype=F32) / n_freq))
    ang_r = (pos // GRID_W).astype(F32)[:, None] * inv
    ang_c = (pos % GRID_W).astype(F32)[:, None] * inv
    zero = jnp.zeros_like(ang_r)
    cos = jnp.concatenate([jnp.cos(ang_r)] * 2 + [jnp.cos(ang_c)] * 2, axis=-1)
    sin_lo = jnp.concatenate([-jnp.sin(ang_r), zero, -jnp.sin(ang_c), zero], axis=-1)
    sin_hi = jnp.concatenate([zero, jnp.sin(ang_r), zero, jnp.sin(ang_c)], axis=-1)
    ident = lambda a, fill: jnp.stack([jnp.full_like(a, fill), a])
    return ident(cos, 1.0), ident(sin_lo, 0.0), ident(sin_hi, 0.0)


def kernel(x_prompt, x_sample, c, state_mlstm_C, state_mlstm_n, state_mlstm_m, cache_k, cache_v,
           c_ctx, w_mod, b_mod, norm1_g, norm2_g, w_in_ab, b_gate_ab, conv_w, conv_b, conv_ln_g,
           conv_ln_b, mlstm_norm_g, w_out_ab, w_in_c, q_norm_g, k_norm_g, w_out_c, w_ffn_in,
           w_ffn_out, final_norm_g):
    n_lat = x_sample.shape[0]
    n_ctx = x_prompt.shape[0]
    xp = x_prompt.reshape(1, GROUP_TOKENS, D_MODEL)

    cond8 = jnp.concatenate([c_ctx[None, :], c, jnp.zeros((8 - 1 - n_lat, D_MODEL), F32)], axis=0)
    mods = _adaln_all(cond8, w_mod, b_mod)[:, :N_GROUPS].reshape(2, N_GROUPS, 6, D_MODEL)
    row = lambda a: a.reshape(1, -1)

    main_cols = 2 * CONV_CH + 4 * M_WIDTH
    w_ab = w_in_ab[0]
    w_main = w_ab[:, :main_cols].astype(BF16)
    wqt = w_ab[:, 2 * CONV_CH:2 * CONV_CH + M_WIDTH].T.astype(BF16)
    wvt = w_ab[:, 2 * CONV_CH + 2 * M_WIDTH:2 * CONV_CH + 3 * M_WIDTH].T.astype(BF16)
    gate_w = w_ab[:, main_cols:].reshape(D_MODEL, 2, 2, M_HEADS)
    gate_b = b_gate_ab[0].reshape(2, 2, M_HEADS)
    pad = 128 - N_STREAMS
    wgi = jnp.pad(gate_w[:, :, 0].reshape(D_MODEL, N_STREAMS), ((0, 0), (0, pad))).astype(BF16)
    wgf_f32 = gate_w[:, :, 1].reshape(D_MODEL, N_STREAMS)
    wgf = jnp.pad(wgf_f32, ((0, 0), (0, pad))).astype(BF16)
    wgft = wgf_f32.T.astype(BF16)
    bgi = jnp.pad(gate_b[:, 0].reshape(N_STREAMS), (0, pad)).reshape(1, 128)
    bgf = jnp.pad(gate_b[:, 1].reshape(N_STREAMS), (0, pad)).reshape(1, 128)
    bgft = gate_b[:, 1].reshape(N_STREAMS, 1)
    u, qt, k, vt, o, gi, gf, gft = _ab_in(xp, x_sample, mods[0], row(norm1_g[0]), w_main, wqt,
                                          wvt, wgi, wgf, wgft, bgi, bgf, bgft)

    c0 = state_mlstm_C[:, 0].reshape(n_lat, N_STREAMS, M_DK, M_DK)
    n0 = state_mlstm_n[:, 0].reshape(n_lat, N_STREAMS, M_DK)
    m0 = jnp.pad(state_mlstm_m[:, 0].reshape(n_lat, N_STREAMS), ((0, 0), (0, pad)))
    hf, hb, new_c, new_n, new_m = _mlstm(qt, k, vt, gi, gf, gft, c0, n0,
                                         m0.reshape(n_lat, 1, 128))

    conv_w_pad = jnp.pad(conv_w[0], ((0, 32 - CONV_WIDTH), (0, 0)))
    w_o = w_out_ab[0].astype(BF16)
    x = _ab_out(u, hf, hb, o, xp, x_sample, mods[0], conv_w_pad, row(conv_b[0]),
                row(conv_ln_g[0]), row(conv_ln_b[0]), row(mlstm_norm_g[0]),
                w_o[:CONV_CH], w_o[CONV_CH:])
    x = _ffn(x, mods[0], row(norm2_g[0]), w_ffn_in[0].astype(BF16), w_ffn_out[0].astype(BF16))

    cos, sin_lo, sin_hi = _rope_tables()
    qa, ka, va, k_norm, v_raw = _c_in(x, mods[1], row(norm1_g[1]), w_in_c[0].astype(BF16),
                                      row(q_norm_g[0]), row(k_norm_g[0]), cos, sin_lo, sin_hi)
    kvw = N_KV * HEAD_DIM
    ck = cache_k[:, 0].reshape(n_lat, PAST_LEN, kvw).astype(BF16)
    cv = cache_v[:, 0].reshape(n_lat, PAST_LEN, kvw).astype(BF16)
    oa = _attention(qa, ka, va, ck, cv)
    x = _c_out(oa, x, mods[1], w_out_c[0].astype(BF16))
    yp, y_sample = _ffn(x, mods[1], row(norm2_g[1]), w_ffn_in[1].astype(BF16),
                        w_ffn_out[1].astype(BF16), row(final_norm_g))

    y_prompt = yp.reshape(n_ctx, CTX_SEQ, D_MODEL)
    new_state_c = new_c.reshape(n_ctx, 1, 2, M_HEADS, M_DK, M_DK)
    new_state_n = new_n.reshape(n_ctx, 1, 2, M_HEADS, M_DK)
    new_state_m = new_m[:, 0, :N_STREAMS].reshape(n_ctx, 1, 2, M_HEADS)
    new_k = k_norm[0].reshape(n_ctx, 1, CTX_SEQ, N_KV, HEAD_DIM)
    new_v = v_raw[0].reshape(n_ctx, 1, CTX_SEQ, N_KV, HEAD_DIM)
    return (y_prompt, y_sample, new_state_c, new_state_n, new_state_m, new_k, new_v)
```

```python
import jax
import jax.numpy as jnp
from jax import lax
from jax.experimental import pallas as pl
from jax.experimental.pallas import tpu as pltpu

F32 = jnp.float32
BF16 = jnp.bfloat16

D_MODEL = 1024
N_GROUPS = 5
GROUP_TOKENS = 4096
CTX_SEQ = 256
CONV_CH = 512
CONV_WIDTH = 31
M_HEADS = 4
M_DK = 128
M_WIDTH = 512
N_STREAMS = 2 * M_HEADS
STATE_ROWS = M_DK + 16
HEAD_DIM = 128
N_Q = 8
N_KV = 2
GROUP = N_Q // N_KV
PAST_LEN = 256
GRID_W = 64
ROPE_THETA = 10000.0
D_FF = 2816
EPS = 1e-6
LOG2_E = 1.4426950408889634

CHUNK = 256
TOKEN_TILE = 512
ROPE_TILE = 256
SEQ_TILE = 256
ATTN_TILE = 512
ATTN_CHAIN_ROWS = 128
FFN_CUTS = (0, 1536, D_FF)
HALO = 16
SUBLANES = 8
VMEM_LIMIT = 56 * 1024 * 1024

_ARB2 = pltpu.CompilerParams(dimension_semantics=("arbitrary", "arbitrary"),
                             vmem_limit_bytes=VMEM_LIMIT)
_ARB3 = pltpu.CompilerParams(dimension_semantics=("arbitrary", "arbitrary", "arbitrary"),
                             vmem_limit_bytes=VMEM_LIMIT)


def _rms(x):
    return x * lax.rsqrt(jnp.mean(x * x, axis=-1, keepdims=True) + EPS)


def _modulated(x, gain, shift, scale):
    return (_rms(x) * gain) * (1.0 + scale) + shift


def _sigmoid(x):
    return 1.0 / (1.0 + jnp.exp(-x))


def _log_sigmoid(x):
    return jnp.minimum(x, 0.0) - jnp.log1p(jnp.exp(-jnp.abs(x)))


def _dot(a, b):
    return jnp.dot(a, b, preferred_element_type=F32)


def _dot_nt(a, b):
    return lax.dot_general(a, b, (((1,), (1,)), ((), ())), preferred_element_type=F32)


def _split3(x):
    hi = x.astype(BF16)
    r1 = x - hi.astype(F32)
    mid = r1.astype(BF16)
    lo = (r1 - mid.astype(F32)).astype(BF16)
    return hi, mid, lo


def _resident_spec(a):
    return pl.BlockSpec(a.shape, lambda *_: (0,) * a.ndim, pipeline_mode=pl.Buffered(1))


def _ctx_spec(tile, nt, width):
    return pl.BlockSpec((None, tile, width), lambda g, t: (0, jnp.where(g == 0, t, nt - 1), 0))


def _lat_spec(tile, width):
    return pl.BlockSpec((None, tile, width),
                        lambda g, t: (jnp.maximum(g - 1, 0), jnp.where(g > 0, t, 0), 0))


def _mod_kernel(cond_ref, w_ref, b_ref, o_ref):
    c = cond_ref[...]
    s = (c * _sigmoid(c)).astype(BF16)
    o_ref[...] = _dot(s, w_ref[...].astype(BF16)) + b_ref[...]


def _adaln_all(cond8, w_mod, b_mod):
    depth, d, n = w_mod.shape
    tn = 1024
    return pl.pallas_call(
        _mod_kernel,
        grid=(depth, n // tn),
        in_specs=[
            pl.BlockSpec((8, d), lambda l, j: (0, 0)),
            pl.BlockSpec((None, d, tn), lambda l, j: (l, 0, j)),
            pl.BlockSpec((None, 1, tn), lambda l, j: (l, 0, j)),
        ],
        out_specs=pl.BlockSpec((None, 8, tn), lambda l, j: (l, 0, j)),
        out_shape=jax.ShapeDtypeStruct((depth, 8, n), F32),
        compiler_params=_ARB2,
        name="adaln_mod",
    )(cond8, w_mod, b_mod.reshape(depth, 1, n))


def _ab_in_kernel(xp_ref, xs_ref, mod_ref, g_ref, w_ref, wqvt_ref,
                  wg_ref, wgft_ref, bg_ref, bgft_ref,
                  u_ref, qt_ref, k_ref, vt_ref, o_ref, gi_ref, gf_ref, gft_ref):
    x = jnp.where(pl.program_id(0) == 0, xp_ref[...], xs_ref[...])
    h = _modulated(x, g_ref[...], mod_ref[0:1, :], mod_ref[1:2, :])
    hb = h.astype(BF16)

    def proj(c):
        return _dot(hb, w_ref[:, c * CONV_CH:(c + 1) * CONV_CH])

    u_ref[...] = proj(0) * _sigmoid(proj(1))
    k_ref[...] = proj(3) * (M_DK ** -0.5)
    o_ref[...] = proj(5)
    qvt = _dot_nt(wqvt_ref[...], hb).astype(BF16)
    qt_ref[...] = qvt[:M_WIDTH]
    vt_ref[...] = qvt[M_WIDTH:]
    lane = lax.broadcasted_iota(jnp.int32, (x.shape[0], 128), 1)
    gates = _dot(hb, wg_ref[...]) + bg_ref[...]
    gi_ref[...] = gates[:, :128]
    gf_ref[...] = jnp.where(lane < N_STREAMS, _log_sigmoid(gates[:, 128:]), 0.0)
    gft_ref[...] = _log_sigmoid(_dot_nt(wgft_ref[...], hb) + bgft_ref[...])


def _ab_in(xp, xs, mod, gain, w_main, wqvt, wg, wgft, bg, bgft):
    tm = TOKEN_TILE
    nt = GROUP_TOKENS // tm
    tok = lambda width: pl.BlockSpec((None, tm, width), lambda g, t: (g, t, 0))
    feat = lambda rows: pl.BlockSpec((None, rows, tm), lambda g, t: (g, 0, t))
    const = _resident_spec
    tok_sds = lambda width: jax.ShapeDtypeStruct((N_GROUPS, GROUP_TOKENS, width), F32)
    feat_sds = lambda rows, dt: jax.ShapeDtypeStruct((N_GROUPS, rows, GROUP_TOKENS), dt)
    consts = (gain, w_main, wqvt, wg, wgft, bg, bgft)
    return pl.pallas_call(
        _ab_in_kernel,
        grid=(N_GROUPS, nt),
        in_specs=[_ctx_spec(tm, nt, D_MODEL), _lat_spec(tm, D_MODEL),
                  pl.BlockSpec((None, 6, D_MODEL), lambda g, t: (g, 0, 0))]
                 + [const(a) for a in consts],
        out_specs=[tok(CONV_CH), feat(M_WIDTH), tok(M_WIDTH), feat(M_WIDTH), tok(M_WIDTH),
                   tok(128), tok(128), feat(N_STREAMS)],
        out_shape=[tok_sds(CONV_CH), feat_sds(M_WIDTH, BF16), tok_sds(M_WIDTH),
                   feat_sds(M_WIDTH, BF16), tok_sds(M_WIDTH),
                   tok_sds(128), tok_sds(128), feat_sds(N_STREAMS, F32)],
        compiler_params=_ARB2,
        name="ab_in_proj",
    )(xp, xs, mod, *consts)


def _mlstm_direction(qt_ref, k_ref, vt_ref, gi_ref, gf_ref, gft_ref, h_ref, cn_s, m_s, d):
    L = k_ref.shape[0]
    r = lax.broadcasted_iota(jnp.int32, (L, L), 0)
    c = lax.broadcasted_iota(jnp.int32, (L, L), 1)
    le = r <= c
    ge = r >= c
    gf = gf_ref[...]
    gft = gft_ref[...]
    tri_col = ge.astype(BF16)
    tri_row = le.astype(BF16)
    p_col = sum(_dot(tri_col, part) for part in _split3(gf))
    p_row = sum(_dot(part, tri_row) for part in _split3(gft))
    tot_col = p_col[L - 1:L, :]
    if d == 0:
        b_col, b_row, mask = p_col, p_row, le
    else:
        b_col = tot_col - p_col + gf
        b_row = p_row[:, L - 1:L] - p_row + gft
        mask = ge
    c_all = gi_ref[...] - b_col
    m_prev_all = m_s[...]
    m_fin = jnp.maximum(m_prev_all, jnp.max(c_all, axis=0, keepdims=True))
    w_all = jnp.exp(c_all - m_fin)
    a_all = jnp.exp(m_prev_all - m_fin)
    lane = lax.broadcasted_iota(jnp.int32, (1, 128), 1)
    mine = (lane >= d * M_HEADS) & (lane < (d + 1) * M_HEADS)
    m_s[...] = jnp.where(mine, tot_col + m_fin, m_prev_all)
    ones_row = jnp.where(lax.broadcasted_iota(jnp.int32, (STATE_ROWS - M_DK, L), 0) == 0,
                         1.0, 0.0).astype(BF16)
    for h in range(M_HEADS):
        s_idx = d * M_HEADS + h
        sl = slice(h * M_DK, (h + 1) * M_DK)
        c_col = c_all[:, s_idx:s_idx + 1]
        m_prev = m_prev_all[:, s_idx:s_idx + 1]
        cm = jnp.where(mask, c_col, -jnp.inf)
        m_row = jnp.maximum(jnp.max(cm, axis=0, keepdims=True), m_prev)
        dmat = jnp.exp(cm - m_row)
        k = k_ref[:, sl]
        cn = cn_s[s_idx]
        lhs = jnp.concatenate([k.astype(BF16), cn.astype(BF16)], axis=0)
        r1 = _dot(lhs, qt_ref[sl, :])
        st = (r1[0:L] * dmat).astype(BF16)
        vt_aug = jnp.concatenate([vt_ref[sl, :], ones_row], axis=0)
        w_inter = jnp.exp(m_prev - m_row)
        tot = _dot(vt_aug, st) + w_inter * r1[L:L + STATE_ROWS]
        den = tot[M_DK:M_DK + 1, :]
        m_t = b_row[s_idx:s_idx + 1, :] + m_row
        ht = tot[0:M_DK, :] / jnp.maximum(jnp.abs(den), jnp.exp(-m_t))
        h_ref[:, sl] = ht.T
        kw = (k * w_all[:, s_idx:s_idx + 1]).astype(BF16)
        cn_s[s_idx] = a_all[:, s_idx:s_idx + 1] * cn + _dot(vt_aug, kw)


def _mlstm_kernel(qtf_ref, kf_ref, vtf_ref, gif_ref, gff_ref, gftf_ref,
                  qtb_ref, kb_ref, vtb_ref, gib_ref, gfb_ref, gftb_ref,
                  c0_ref, n0_ref, m0_ref,
                  hf_ref, hb_ref, co_ref, no_ref, mo_ref,
                  cn_s, m_s):
    g = pl.program_id(0)
    c = pl.program_id(1)

    @pl.when(g == 0)
    def _():
        cn_s[...] = jnp.zeros_like(cn_s)
        m_s[...] = jnp.zeros_like(m_s)

    @pl.when((g > 0) & (c == 0))
    def _():
        cn_s[...] = jnp.zeros_like(cn_s)
        for s in range(N_STREAMS):
            cn_s[s, 0:M_DK, :] = c0_ref[s].T
            cn_s[s, M_DK:M_DK + 1, :] = n0_ref[s:s + 1, :]
        m_s[...] = m0_ref[...]

    _mlstm_direction(qtf_ref, kf_ref, vtf_ref, gif_ref, gff_ref, gftf_ref, hf_ref, cn_s, m_s, 0)
    _mlstm_direction(qtb_ref, kb_ref, vtb_ref, gib_ref, gfb_ref, gftb_ref, hb_ref, cn_s, m_s, 1)

    @pl.when(g == 0)
    def _():
        for s in range(N_STREAMS):
            co_ref[s] = cn_s[s, 0:M_DK, :].T
            no_ref[s:s + 1, :] = cn_s[s, M_DK:M_DK + 1, :]
        mo_ref[...] = m_s[...]


def _mlstm(qt, k, vt, gi, gf, gft, c0, n0, m0):
    L = CHUNK
    nc = GROUP_TOKENS // L
    n_ctx = GROUP_TOKENS // CTX_SEQ
    assert CTX_SEQ == L and n_ctx == nc
    fwd = lambda g, c: c
    bwd = lambda g, c: jnp.where(g == 0, c, nc - 1 - c)
    tok = lambda width, pos: pl.BlockSpec((None, L, width), lambda g, c: (g, pos(g, c), 0))
    feat = lambda rows, pos: pl.BlockSpec((None, rows, L), lambda g, c: (g, 0, pos(g, c)))
    lat = lambda g, c: jnp.maximum(g - 1, 0)
    ctx = lambda g, c: jnp.where(g == 0, c, n_ctx - 1)
    per_dir = lambda pos: [feat(M_WIDTH, pos), tok(M_WIDTH, pos), feat(M_WIDTH, pos),
                           tok(128, pos), tok(128, pos), feat(N_STREAMS, pos)]
    return pl.pallas_call(
        _mlstm_kernel,
        grid=(N_GROUPS, nc),
        in_specs=per_dir(fwd) + per_dir(bwd) + [
            pl.BlockSpec((None, N_STREAMS, M_DK, M_DK), lambda g, c: (lat(g, c), 0, 0, 0)),
            pl.BlockSpec((None, N_STREAMS, M_DK), lambda g, c: (lat(g, c), 0, 0)),
            pl.BlockSpec((None, 1, 128), lambda g, c: (lat(g, c), 0, 0)),
        ],
        out_specs=[
            tok(M_WIDTH, fwd), tok(M_WIDTH, bwd),
            pl.BlockSpec((None, N_STREAMS, M_DK, M_DK), lambda g, c: (ctx(g, c), 0, 0, 0)),
            pl.BlockSpec((None, N_STREAMS, M_DK), lambda g, c: (ctx(g, c), 0, 0)),
            pl.BlockSpec((None, 1, 128), lambda g, c: (ctx(g, c), 0, 0)),
        ],
        out_shape=[
            jax.ShapeDtypeStruct((N_GROUPS, GROUP_TOKENS, M_WIDTH), F32),
            jax.ShapeDtypeStruct((N_GROUPS, GROUP_TOKENS, M_WIDTH), F32),
            jax.ShapeDtypeStruct((n_ctx, N_STREAMS, M_DK, M_DK), F32),
            jax.ShapeDtypeStruct((n_ctx, N_STREAMS, M_DK), F32),
            jax.ShapeDtypeStruct((n_ctx, 1, 128), F32),
        ],
        scratch_shapes=[
            pltpu.VMEM((N_STREAMS, STATE_ROWS, M_DK), F32),
            pltpu.VMEM((1, 128), F32),
        ],
        compiler_params=_ARB2,
        name="mlstm_scan",
    )(qt, k, vt, gi, gf, gft, qt, k, vt, gi, gf, gft, c0, n0, m0)


def _conv_mix_stage(g, t, nt, up_ref, uc_ref, un_ref, hf_ref, hb_ref, o_ref,
                    cw_ref, cb_ref, lg_ref, lb_ref, mg_ref, ext_ref, conv_ref):
    T = uc_ref.shape[0]
    prev_on = jnp.where((g > 0) & (t > 0), 1.0, 0.0).astype(F32)
    next_on = jnp.where((g > 0) & (t < nt - 1), 1.0, 0.0).astype(F32)
    ext_ref[0, 0:HALO, :] = up_ref[...] * prev_on
    ext_ref[0, HALO:HALO + T, :] = uc_ref[...]
    ext_ref[0, HALO + T:HALO + T + HALO, :] = un_ref[...] * next_on
    span = T + 2 * HALO - SUBLANES
    for s in range(1, SUBLANES):
        ext_ref[s, 0:span, :] = ext_ref[0, s:s + span, :]

    groups = 4
    off = HALO - CONV_WIDTH // 2
    for r0 in range(0, T, groups * SUBLANES):
        accs = [cb_ref[...]] * groups
        for j in range(CONV_WIDTH):
            phase, base = (off + j) % SUBLANES, (off + j) // SUBLANES * SUBLANES
            w = cw_ref[j]
            for q in range(groups):
                lo = r0 + base + q * SUBLANES
                accs[q] = accs[q] + w * ext_ref[phase, lo:lo + SUBLANES, :]
        for q in range(groups):
            conv_ref[r0 + q * SUBLANES:r0 + (q + 1) * SUBLANES, :] = accs[q]

    y = conv_ref[...]
    mu = jnp.mean(y, axis=-1, keepdims=True)
    yc = y - mu
    var = jnp.mean(yc * yc, axis=-1, keepdims=True)
    y = yc * lax.rsqrt(var + EPS) * lg_ref[...] + lb_ref[...]
    u = y * _sigmoid(y)

    hm = hf_ref[...] + hb_ref[...]
    hm = jnp.concatenate(
        [_rms(hm[:, h * M_DK:(h + 1) * M_DK]) for h in range(M_HEADS)], axis=-1)
    hm = _sigmoid(o_ref[...]) * (hm * mg_ref[...])
    return u.astype(BF16), hm.astype(BF16)


def _ffn_tail(x, mod_ref, g_ref, win_ref, wout_ref):
    hb = _modulated(x, g_ref[...], mod_ref[3:4, :], mod_ref[4:5, :]).astype(BF16)
    acc = jnp.zeros(x.shape, F32)
    for lo, hi in zip(FFN_CUTS[:-1], FFN_CUTS[1:]):
        gt = _dot(hb, win_ref[:, lo:hi])
        up = _dot(hb, win_ref[:, D_FF + lo:D_FF + hi])
        act = (gt * _sigmoid(gt) * up).astype(BF16)
        acc = acc + _dot(act, wout_ref[lo:hi, :])
    return x + mod_ref[5:6, :] * acc


def _l0_tail_kernel(up_ref, uc_ref, un_ref, hf_ref, hb_ref, o_ref, xp_ref, xs_ref, mod_ref,
                    cw_ref, cb_ref, lg_ref, lb_ref, mg_ref, wu_ref, wm_ref,
                    g2_ref, win_ref, wout_ref,
                    out_ref, ext_ref, conv_ref):
    g = pl.program_id(0)
    t = pl.program_id(1)
    u, hm = _conv_mix_stage(g, t, pl.num_programs(1), up_ref, uc_ref, un_ref, hf_ref, hb_ref,
                            o_ref, cw_ref, cb_ref, lg_ref, lb_ref, mg_ref, ext_ref, conv_ref)
    x = jnp.where(g == 0, xp_ref[...], xs_ref[...])
    out = _dot(u, wu_ref[...]) + _dot(hm, wm_ref[...])
    out_ref[...] = _ffn_tail(x + mod_ref[2:3, :] * out, mod_ref, g2_ref, win_ref, wout_ref)


def _l0_tail(u, hf, hb, o, xp, xs, mod, conv_w, conv_b, ln_g, ln_b, mnorm_g, w_u, w_m,
             gain2, w_in, w_out):
    T = SEQ_TILE
    nt = GROUP_TOKENS // T
    per = T // HALO
    nh = GROUP_TOKENS // HALO
    tok = lambda width: pl.BlockSpec((None, T, width), lambda g, t: (g, t, 0))
    consts = (conv_w, conv_b, ln_g, ln_b, mnorm_g, w_u, w_m, gain2, w_in, w_out)
    return pl.pallas_call(
        _l0_tail_kernel,
        grid=(N_GROUPS, nt),
        in_specs=[
            pl.BlockSpec((None, HALO, CONV_CH),
                         lambda g, t: (g, jnp.maximum(t * per - 1, 0), 0)),
            tok(CONV_CH),
            pl.BlockSpec((None, HALO, CONV_CH),
                         lambda g, t: (g, jnp.minimum((t + 1) * per, nh - 1), 0)),
            tok(M_WIDTH), tok(M_WIDTH), tok(M_WIDTH),
            _ctx_spec(T, nt, D_MODEL), _lat_spec(T, D_MODEL),
            pl.BlockSpec((None, 6, D_MODEL), lambda g, t: (g, 0, 0)),
        ] + [_resident_spec(a) for a in consts],
        out_specs=tok(D_MODEL),
        out_shape=jax.ShapeDtypeStruct((N_GROUPS, GROUP_TOKENS, D_MODEL), F32),
        scratch_shapes=[pltpu.VMEM((SUBLANES, T + 2 * HALO, CONV_CH), F32),
                        pltpu.VMEM((T, CONV_CH), F32)],
        compiler_params=_ARB2,
        name="l0_mix_ffn",
    )(u, u, u, hf, hb, o, xp, xs, mod, *consts)


def _l1_tail_kernel(o_ref, x_ref, mod_ref, wc_ref, g2_ref, win_ref, wout_ref, fg_ref,
                    yp_ref, ys_ref):
    x = x_ref[...] + mod_ref[2:3, :] * _dot(o_ref[...], wc_ref[...])
    y = _rms(_ffn_tail(x, mod_ref, g2_ref, win_ref, wout_ref)) * fg_ref[...]
    g = pl.program_id(0)

    @pl.when(g == 0)
    def _():
        yp_ref[...] = y

    @pl.when(g > 0)
    def _():
        ys_ref[...] = y


def _l1_tail(o, x, mod, w_c, gain2, w_in, w_out, final_gain):
    tm = TOKEN_TILE
    nt = GROUP_TOKENS // tm
    tok = pl.BlockSpec((None, tm, D_MODEL), lambda g, t: (g, t, 0))
    consts = (w_c, gain2, w_in, w_out, final_gain)
    return pl.pallas_call(
        _l1_tail_kernel,
        grid=(N_GROUPS, nt),
        in_specs=[tok, tok, pl.BlockSpec((None, 6, D_MODEL), lambda g, t: (g, 0, 0))]
                 + [_resident_spec(a) for a in consts],
        out_specs=[_ctx_spec(tm, nt, D_MODEL), _lat_spec(tm, D_MODEL)],
        out_shape=[jax.ShapeDtypeStruct((1, GROUP_TOKENS, D_MODEL), F32),
                   jax.ShapeDtypeStruct((N_GROUPS - 1, GROUP_TOKENS, D_MODEL), F32)],
        compiler_params=_ARB2,
        name="l1_proj_ffn",
    )(o, x, mod, *consts)


def _rope(x, cos, sin_lo, sin_hi):
    quarter = HEAD_DIM // 4
    return (x * cos + pltpu.roll(x, HEAD_DIM - quarter, 1) * sin_lo
            + pltpu.roll(x, quarter, 1) * sin_hi)


def _c_in_kernel(x_ref, mod_ref, g_ref, w_ref, qg_ref, kg_ref, cos_ref, slo_ref, shi_ref,
                 q_ref, k_ref, v_ref, kn_ref, vn_ref):
    hb = _modulated(x_ref[...], g_ref[...], mod_ref[0:1, :], mod_ref[1:2, :]).astype(BF16)
    latent = pl.program_id(0) > 0
    cos = jnp.where(latent, cos_ref[...], 1.0)
    slo = jnp.where(latent, slo_ref[...], 0.0)
    shi = jnp.where(latent, shi_ref[...], 0.0)
    q = _dot(hb, w_ref[:, 0:N_Q * HEAD_DIM])
    qg = qg_ref[...] * (HEAD_DIM ** -0.5 * LOG2_E)
    for h in range(N_Q):
        lanes = slice(h * HEAD_DIM, (h + 1) * HEAD_DIM)
        q_ref[:, lanes] = _rope(_rms(q[:, lanes]) * qg, cos, slo, shi).astype(BF16)
    kv = _dot(hb, w_ref[:, N_Q * HEAD_DIM:(N_Q + 2 * N_KV) * HEAD_DIM])
    kg = kg_ref[...]
    for h in range(N_KV):
        lanes = slice(h * HEAD_DIM, (h + 1) * HEAD_DIM)
        kn = _rms(kv[:, lanes]) * kg
        kn_ref[:, lanes] = kn
        k_ref[:, lanes] = _rope(kn, cos, slo, shi).astype(BF16)
    v = kv[:, N_KV * HEAD_DIM:]
    vn_ref[...] = v
    v_ref[...] = v.astype(BF16)


def _c_in(x, mod, gain, w, qg, kg, cos, sin_lo, sin_hi):
    tm = ROPE_TILE
    kvw = N_KV * HEAD_DIM
    tok = lambda width: pl.BlockSpec((None, tm, width), lambda g, t: (g, t, 0))
    const = _resident_spec
    table = pl.BlockSpec((tm, HEAD_DIM), lambda g, t: (t, 0))
    sds = lambda width, dt: jax.ShapeDtypeStruct((N_GROUPS, GROUP_TOKENS, width), dt)
    return pl.pallas_call(
        _c_in_kernel,
        grid=(N_GROUPS, GROUP_TOKENS // tm),
        in_specs=[tok(D_MODEL), pl.BlockSpec((None, 6, D_MODEL), lambda g, t: (g, 0, 0)),
                  const(gain), const(w), const(qg), const(kg), table, table, table],
        out_specs=[tok(N_Q * HEAD_DIM), tok(kvw), tok(kvw), tok(kvw), tok(kvw)],
        out_shape=[sds(N_Q * HEAD_DIM, BF16), sds(kvw, BF16), sds(kvw, BF16),
                   sds(kvw, F32), sds(kvw, F32)],
        compiler_params=_ARB2,
        name="attn_in_proj",
    )(x, mod, gain, w, qg, kg, cos, sin_lo, sin_hi)


def _group_softmax_pv(q_ref, o_ref, rows, k, v_aug):
    for i in range(GROUP):
        lanes = slice(i * HEAD_DIM, (i + 1) * HEAD_DIM)
        s = _dot_nt(q_ref[rows, lanes], k)
        p = jnp.exp2(s - jnp.max(s, axis=-1, keepdims=True)).astype(BF16)
        o = _dot(p, v_aug)
        o_ref[rows, lanes] = (o[:, :HEAD_DIM] / o[:, HEAD_DIM:HEAD_DIM + 1]).astype(BF16)


def _ones_column(rows):
    lane = lax.broadcasted_iota(jnp.int32, (rows, HEAD_DIM), 1)
    return jnp.where(lane == 0, 1.0, 0.0).astype(BF16)


def _attn_kernel(q_ref, k_ref, v_ref, kc_ref, vc_ref, o_ref, kall_ref, vall_ref):
    g = pl.program_id(0)
    t = pl.program_id(2)
    T = q_ref.shape[0]
    S = k_ref.shape[0]

    @pl.when(g == 0)
    def _():
        for j in range(T // CTX_SEQ):
            keys = pl.ds(pl.multiple_of(t * T + j * CTX_SEQ, CTX_SEQ), CTX_SEQ)
            v_aug = jnp.concatenate([v_ref[keys, :], _ones_column(CTX_SEQ)], axis=-1)
            _group_softmax_pv(q_ref, o_ref, slice(j * CTX_SEQ, (j + 1) * CTX_SEQ),
                              k_ref[keys, :], v_aug)

    @pl.when((g > 0) & (t == 0))
    def _():
        kall_ref[0:S, :] = k_ref[...]
        kall_ref[S:, :] = kc_ref[...]
        vall_ref[0:S, 0:HEAD_DIM] = v_ref[...]
        vall_ref[S:, 0:HEAD_DIM] = vc_ref[...]
        vall_ref[:, HEAD_DIM:] = _ones_column(vall_ref.shape[0])

    @pl.when(g > 0)
    def _():
        for r0 in range(0, T, ATTN_CHAIN_ROWS):
            _group_softmax_pv(q_ref, o_ref, slice(r0, r0 + ATTN_CHAIN_ROWS),
                              kall_ref[...], vall_ref[...])


def _attention(q, k, v, cache_k, cache_v):
    T = ATTN_TILE
    assert T % CTX_SEQ == 0
    gw = GROUP * HEAD_DIM
    qspec = pl.BlockSpec((None, T, gw), lambda g, h, t: (g, t, h))
    kspec = pl.BlockSpec((None, GROUP_TOKENS, HEAD_DIM), lambda g, h, t: (g, 0, h))
    cspec = pl.BlockSpec((None, PAST_LEN, HEAD_DIM),
                         lambda g, h, t: (jnp.maximum(g - 1, 0), 0, h))
    n_keys = GROUP_TOKENS + PAST_LEN
    return pl.pallas_call(
        _attn_kernel,
        grid=(N_GROUPS, N_KV, GROUP_TOKENS // T),
        in_specs=[qspec, kspec, kspec, cspec, cspec],
        out_specs=qspec,
        out_shape=jax.ShapeDtypeStruct(q.shape, BF16),
        scratch_shapes=[pltpu.VMEM((n_keys, HEAD_DIM), BF16),
                        pltpu.VMEM((n_keys, 2 * HEAD_DIM), BF16)],
        compiler_params=_ARB3,
        name="gqa_attention",
    )(q, k, v, cache_k, cache_v)


def _rope_tables():
    rows = GROUP_TOKENS // GRID_W
    assert rows == GRID_W
    n_freq = HEAD_DIM // 4
    inv = 1.0 / (ROPE_THETA ** (jnp.arange(n_freq, dtype=F32) / n_freq))
    ang = jnp.arange(GRID_W).astype(F32)[:, None] * inv
    by_row = lambda a: jnp.repeat(a, GRID_W, axis=0)
    by_col = lambda a: jnp.tile(a, (rows, 1))
    cos_r, cos_c = by_row(jnp.cos(ang)), by_col(jnp.cos(ang))
    sin_r, sin_c = by_row(jnp.sin(ang)), by_col(jnp.sin(ang))
    zero = jnp.zeros_like(cos_r)
    cos = jnp.concatenate([cos_r, cos_r, cos_c, cos_c], axis=-1)
    sin_lo = jnp.concatenate([-sin_r, zero, -sin_c, zero], axis=-1)
    sin_hi = jnp.concatenate([zero, sin_r, zero, sin_c], axis=-1)
    return cos, sin_lo, sin_hi


def kernel(x_prompt, x_sample, c, state_mlstm_C, state_mlstm_n, state_mlstm_m, cache_k, cache_v,
           c_ctx, w_mod, b_mod, norm1_g, norm2_g, w_in_ab, b_gate_ab, conv_w, conv_b, conv_ln_g,
           conv_ln_b, mlstm_norm_g, w_out_ab, w_in_c, q_norm_g, k_norm_g, w_out_c, w_ffn_in,
           w_ffn_out, final_norm_g):
    n_lat = x_sample.shape[0]
    n_ctx = x_prompt.shape[0]
    xp = x_prompt.reshape(1, GROUP_TOKENS, D_MODEL)

    cond8 = jnp.concatenate([c_ctx[None, :], c, jnp.zeros((8 - 1 - n_lat, D_MODEL), F32)], axis=0)
    mods = _adaln_all(cond8, w_mod, b_mod)[:, :N_GROUPS].reshape(2, N_GROUPS, 6, D_MODEL)
    row = lambda a: a.reshape(1, -1)

    main_cols = 2 * CONV_CH + 4 * M_WIDTH
    w_ab = w_in_ab[0]
    w_main = w_ab[:, :main_cols].astype(BF16)
    q_cols = slice(2 * CONV_CH, 2 * CONV_CH + M_WIDTH)
    v_cols = slice(2 * CONV_CH + 2 * M_WIDTH, 2 * CONV_CH + 3 * M_WIDTH)
    wqvt = jnp.concatenate([w_ab[:, q_cols], w_ab[:, v_cols]], axis=1).T.astype(BF16)
    gate_w = w_ab[:, main_cols:].reshape(D_MODEL, 2, 2, M_HEADS)
    gate_b = b_gate_ab[0].reshape(2, 2, M_HEADS)
    pad = 128 - N_STREAMS
    lanes = lambda a: jnp.pad(a.reshape(-1, N_STREAMS), ((0, 0), (0, pad)))
    wg = jnp.concatenate([lanes(gate_w[:, :, 0]), lanes(gate_w[:, :, 1])], axis=1).astype(BF16)
    bg = jnp.concatenate([lanes(gate_b[:, 0]), lanes(gate_b[:, 1])], axis=1)
    wgft = gate_w[:, :, 1].reshape(D_MODEL, N_STREAMS).T.astype(BF16)
    bgft = gate_b[:, 1].reshape(N_STREAMS, 1)
    u, qt, k, vt, o, gi, gf, gft = _ab_in(xp, x_sample, mods[0], row(norm1_g[0]), w_main, wqvt,
                                          wg, wgft, bg, bgft)

    c0 = state_mlstm_C[:, 0].reshape(n_lat, N_STREAMS, M_DK, M_DK)
    n0 = state_mlstm_n[:, 0].reshape(n_lat, N_STREAMS, M_DK)
    m0 = jnp.pad(state_mlstm_m[:, 0].reshape(n_lat, N_STREAMS), ((0, 0), (0, pad)))
    hf, hb, new_c, new_n, new_m = _mlstm(qt, k, vt, gi, gf, gft, c0, n0,
                                         m0.reshape(n_lat, 1, 128))

    conv_w_rep = jnp.broadcast_to(conv_w[0][:, None, :], (CONV_WIDTH, SUBLANES, CONV_CH))
    conv_b_rep = jnp.broadcast_to(conv_b[0][None, :], (SUBLANES, CONV_CH))
    w_o = w_out_ab[0].astype(BF16)
    x = _l0_tail(u, hf, hb, o, xp, x_sample, mods[0], conv_w_rep, conv_b_rep,
                 row(conv_ln_g[0]), row(conv_ln_b[0]), row(mlstm_norm_g[0]),
                 w_o[:CONV_CH], w_o[CONV_CH:], row(norm2_g[0]),
                 w_ffn_in[0].astype(BF16), w_ffn_out[0].astype(BF16))

    cos, sin_lo, sin_hi = _rope_tables()
    qa, ka, va, k_norm, v_raw = _c_in(x, mods[1], row(norm1_g[1]), w_in_c[0].astype(BF16),
                                      row(q_norm_g[0]), row(k_norm_g[0]), cos, sin_lo, sin_hi)
    kvw = N_KV * HEAD_DIM
    ck = cache_k[:, 0].reshape(n_lat, PAST_LEN, kvw).astype(BF16)
    cv = cache_v[:, 0].reshape(n_lat, PAST_LEN, kvw).astype(BF16)
    oa = _attention(qa, ka, va, ck, cv)
    yp, y_sample = _l1_tail(oa, x, mods[1], w_out_c[0].astype(BF16), row(norm2_g[1]),
                            w_ffn_in[1].astype(BF16), w_ffn_out[1].astype(BF16),
                            row(final_norm_g))

    y_prompt = yp.reshape(n_ctx, CTX_SEQ, D_MODEL)
    new_state_c = new_c.reshape(n_ctx, 1, 2, M_HEADS, M_DK, M_DK)
    new_state_n = new_n.reshape(n_ctx, 1, 2, M_HEADS, M_DK)
    new_state_m = new_m[:, 0, :N_STREAMS].reshape(n_ctx, 1, 2, M_HEADS)
    new_k = k_norm[0].reshape(n_ctx, 1, CTX_SEQ, N_KV, HEAD_DIM)
    new_v = v_raw[0].reshape(n_ctx, 1, CTX_SEQ, N_KV, HEAD_DIM)
    return (y_prompt, y_sample, new_state_c, new_state_n, new_state_m, new_k, new_v)
```

```python
import jax
import jax.numpy as jnp
from jax import lax
from jax.experimental import pallas as pl
from jax.experimental.pallas import tpu as pltpu

F32 = jnp.float32
BF16 = jnp.bfloat16

D_MODEL = 1024
N_GROUPS = 5
GROUP_TOKENS = 4096
CTX_SEQ = 256
CONV_CH = 512
CONV_WIDTH = 31
M_HEADS = 4
M_DK = 128
M_WIDTH = 512
N_STREAMS = 2 * M_HEADS
STATE_ROWS = M_DK + 16
HEAD_DIM = 128
N_Q = 8
N_KV = 2
GROUP = N_Q // N_KV
PAST_LEN = 256
GRID_W = 64
ROPE_THETA = 10000.0
D_FF = 2816
EPS = 1e-6
LOG2_E = 1.4426950408889634

CHUNK = 256
TOKEN_TILE = 512
ROPE_TILE = 256
SEQ_TILE = 256
ATTN_TILE = 512
ATTN_CHAIN_ROWS = 128
FFN_CUTS = (0, 1536, D_FF)
FFN_CHAIN_ROWS = 512
HALO = 16
SUBLANES = 8
VMEM_LIMIT = 56 * 1024 * 1024

_ARB2 = pltpu.CompilerParams(dimension_semantics=("arbitrary", "arbitrary"),
                             vmem_limit_bytes=VMEM_LIMIT)
_ARB3 = pltpu.CompilerParams(dimension_semantics=("arbitrary", "arbitrary", "arbitrary"),
                             vmem_limit_bytes=VMEM_LIMIT)


def _rms(x):
    return x * lax.rsqrt(jnp.mean(x * x, axis=-1, keepdims=True) + EPS)


def _modulated(x, gain, shift, scale):
    return (_rms(x) * gain) * (1.0 + scale) + shift


def _sigmoid(x):
    return 1.0 / (1.0 + jnp.exp(-x))


def _log_sigmoid(x):
    return jnp.minimum(x, 0.0) - jnp.log1p(jnp.exp(-jnp.abs(x)))


def _dot(a, b):
    return jnp.dot(a, b, preferred_element_type=F32)


def _dot_nt(a, b):
    return lax.dot_general(a, b, (((1,), (1,)), ((), ())), preferred_element_type=F32)


def _prefix_sum(x, axis):
    pos = lax.broadcasted_iota(jnp.int32, x.shape, axis)
    k = 1
    while k < x.shape[axis]:
        x = x + jnp.where(pos >= k, pltpu.roll(x, k, axis), 0.0)
        k *= 2
    return x


def _resident_spec(a):
    return pl.BlockSpec(a.shape, lambda *_: (0,) * a.ndim, pipeline_mode=pl.Buffered(1))


def _ctx_spec(tile, nt, width):
    return pl.BlockSpec((None, tile, width), lambda g, t: (0, jnp.where(g == 0, t, nt - 1), 0))


def _lat_spec(tile, width):
    return pl.BlockSpec((None, tile, width),
                        lambda g, t: (jnp.maximum(g - 1, 0), jnp.where(g > 0, t, 0), 0))


def _mod_kernel(cond_ref, w_ref, b_ref, o_ref):
    c = cond_ref[...]
    s = (c * _sigmoid(c)).astype(BF16)
    o_ref[...] = _dot(s, w_ref[...].astype(BF16)) + b_ref[...]


def _adaln_all(cond8, w_mod, b_mod):
    depth, d, n = w_mod.shape
    tn = 1024
    return pl.pallas_call(
        _mod_kernel,
        grid=(depth, n // tn),
        in_specs=[
            pl.BlockSpec((8, d), lambda l, j: (0, 0)),
            pl.BlockSpec((None, d, tn), lambda l, j: (l, 0, j)),
            pl.BlockSpec((None, 1, tn), lambda l, j: (l, 0, j)),
        ],
        out_specs=pl.BlockSpec((None, 8, tn), lambda l, j: (l, 0, j)),
        out_shape=jax.ShapeDtypeStruct((depth, 8, n), F32),
        compiler_params=_ARB2,
        name="adaln_mod",
    )(cond8, w_mod, b_mod.reshape(depth, 1, n))


def _ab_in_kernel(xp_ref, xs_ref, mod_ref, g_ref, w_ref, wqvt_ref,
                  wg_ref, wgft_ref, bg_ref, bgft_ref,
                  u_ref, qt_ref, k_ref, vt_ref, o_ref, gi_ref, gf_ref, gft_ref):
    x = jnp.where(pl.program_id(0) == 0, xp_ref[...], xs_ref[...])
    h = _modulated(x, g_ref[...], mod_ref[0:1, :], mod_ref[1:2, :])
    hb = h.astype(BF16)

    def proj(c):
        return _dot(hb, w_ref[:, c * CONV_CH:(c + 1) * CONV_CH])

    u_ref[...] = proj(0) * _sigmoid(proj(1))
    k_ref[...] = proj(3) * (M_DK ** -0.5)
    o_ref[...] = proj(5)
    qvt = _dot_nt(wqvt_ref[...], hb).astype(BF16)
    qt_ref[...] = qvt[:M_WIDTH]
    vt_ref[...] = qvt[M_WIDTH:]
    lane = lax.broadcasted_iota(jnp.int32, (x.shape[0], 128), 1)
    gates = _dot(hb, wg_ref[...]) + bg_ref[...]
    gi_ref[...] = gates[:, :128]
    gf_ref[...] = jnp.where(lane < N_STREAMS, _log_sigmoid(gates[:, 128:]), 0.0)
    gft_ref[...] = _log_sigmoid(_dot_nt(wgft_ref[...], hb) + bgft_ref[...])


def _ab_in(xp, xs, mod, gain, w_main, wqvt, wg, wgft, bg, bgft):
    tm = TOKEN_TILE
    nt = GROUP_TOKENS // tm
    tok = lambda width: pl.BlockSpec((None, tm, width), lambda g, t: (g, t, 0))
    feat = lambda rows: pl.BlockSpec((None, rows, tm), lambda g, t: (g, 0, t))
    const = _resident_spec
    tok_sds = lambda width: jax.ShapeDtypeStruct((N_GROUPS, GROUP_TOKENS, width), F32)
    feat_sds = lambda rows, dt: jax.ShapeDtypeStruct((N_GROUPS, rows, GROUP_TOKENS), dt)
    consts = (gain, w_main, wqvt, wg, wgft, bg, bgft)
    return pl.pallas_call(
        _ab_in_kernel,
        grid=(N_GROUPS, nt),
        in_specs=[_ctx_spec(tm, nt, D_MODEL), _lat_spec(tm, D_MODEL),
                  pl.BlockSpec((None, 6, D_MODEL), lambda g, t: (g, 0, 0))]
                 + [const(a) for a in consts],
        out_specs=[tok(CONV_CH), feat(M_WIDTH), tok(M_WIDTH), feat(M_WIDTH), tok(M_WIDTH),
                   tok(128), tok(128), feat(N_STREAMS)],
        out_shape=[tok_sds(CONV_CH), feat_sds(M_WIDTH, BF16), tok_sds(M_WIDTH),
                   feat_sds(M_WIDTH, BF16), tok_sds(M_WIDTH),
                   tok_sds(128), tok_sds(128), feat_sds(N_STREAMS, F32)],
        compiler_params=_ARB2,
        name="ab_in_proj",
    )(xp, xs, mod, *consts)


def _mlstm_direction(qt_ref, k_ref, vt_ref, gi_ref, gf_ref, gft_ref, h_ref, cn_s, m_s, d):
    L = k_ref.shape[0]
    r = lax.broadcasted_iota(jnp.int32, (L, L), 0)
    c = lax.broadcasted_iota(jnp.int32, (L, L), 1)
    le = r <= c
    ge = r >= c
    gf = gf_ref[...]
    gft = gft_ref[...]
    p_col = _prefix_sum(gf, 0)
    p_row = _prefix_sum(gft, 1)
    tot_col = p_col[L - 1:L, :]
    if d == 0:
        b_col, b_row, mask = p_col, p_row, le
    else:
        b_col = tot_col - p_col + gf
        b_row = p_row[:, L - 1:L] - p_row + gft
        mask = ge
    c_all = gi_ref[...] - b_col
    m_prev_all = m_s[...]
    m_fin = jnp.maximum(m_prev_all, jnp.max(c_all, axis=0, keepdims=True))
    w_all = jnp.exp(c_all - m_fin)
    a_all = jnp.exp(m_prev_all - m_fin)
    lane = lax.broadcasted_iota(jnp.int32, (1, 128), 1)
    mine = (lane >= d * M_HEADS) & (lane < (d + 1) * M_HEADS)
    m_s[...] = jnp.where(mine, tot_col + m_fin, m_prev_all)
    ones_row = jnp.where(lax.broadcasted_iota(jnp.int32, (STATE_ROWS - M_DK, L), 0) == 0,
                         1.0, 0.0).astype(BF16)
    for h in range(M_HEADS):
        s_idx = d * M_HEADS + h
        sl = slice(h * M_DK, (h + 1) * M_DK)
        c_col = c_all[:, s_idx:s_idx + 1]
        m_prev = m_prev_all[:, s_idx:s_idx + 1]
        cm = jnp.where(mask, c_col, -jnp.inf)
        m_row = jnp.maximum(jnp.max(cm, axis=0, keepdims=True), m_prev)
        dmat = jnp.exp(cm - m_row)
        k = k_ref[:, sl]
        cn = cn_s[s_idx]
        lhs = jnp.concatenate([k.astype(BF16), cn.astype(BF16)], axis=0)
        r1 = _dot(lhs, qt_ref[sl, :])
        st = (r1[0:L] * dmat).astype(BF16)
        vt_aug = jnp.concatenate([vt_ref[sl, :], ones_row], axis=0)
        w_inter = jnp.exp(m_prev - m_row)
        tot = _dot(vt_aug, st) + w_inter * r1[L:L + STATE_ROWS]
        den = tot[M_DK:M_DK + 1, :]
        m_t = b_row[s_idx:s_idx + 1, :] + m_row
        ht = tot[0:M_DK, :] / jnp.maximum(jnp.abs(den), jnp.exp(-m_t))
        h_ref[:, sl] = ht.T
        kw = (k * w_all[:, s_idx:s_idx + 1]).astype(BF16)
        cn_s[s_idx] = a_all[:, s_idx:s_idx + 1] * cn + _dot(vt_aug, kw)


def _mlstm_kernel(qtf_ref, kf_ref, vtf_ref, gif_ref, gff_ref, gftf_ref,
                  qtb_ref, kb_ref, vtb_ref, gib_ref, gfb_ref, gftb_ref,
                  c0_ref, n0_ref, m0_ref,
                  up_ref, uc_ref, un_ref, cw_ref, cb_ref, lg_ref, lb_ref,
                  hf_ref, hb_ref, co_ref, no_ref, mo_ref, uconv_ref,
                  cn_s, m_s, ext_ref, conv_ref):
    g = pl.program_id(0)
    c = pl.program_id(1)


    @pl.when(g == 0)
    def _():
        cn_s[...] = jnp.zeros_like(cn_s)
        m_s[...] = jnp.zeros_like(m_s)

    @pl.when((g > 0) & (c == 0))
    def _():
        cn_s[...] = jnp.zeros_like(cn_s)
        for s in range(N_STREAMS):
            cn_s[s, 0:M_DK, :] = c0_ref[s].T
            cn_s[s, M_DK:M_DK + 1, :] = n0_ref[s:s + 1, :]
        m_s[...] = m0_ref[...]

    _mlstm_direction(qtf_ref, kf_ref, vtf_ref, gif_ref, gff_ref, gftf_ref, hf_ref, cn_s, m_s, 0)
    _mlstm_direction(qtb_ref, kb_ref, vtb_ref, gib_ref, gfb_ref, gftb_ref, hb_ref, cn_s, m_s, 1)
    uconv_ref[...] = _conv_stage(g, c, pl.num_programs(1), up_ref, uc_ref, un_ref,
                                 cw_ref, cb_ref, lg_ref, lb_ref, ext_ref, conv_ref)

    @pl.when(g == 0)
    def _():
        for s in range(N_STREAMS):
            co_ref[s] = cn_s[s, 0:M_DK, :].T
            no_ref[s:s + 1, :] = cn_s[s, M_DK:M_DK + 1, :]
        mo_ref[...] = m_s[...]


def _mlstm(qt, k, vt, gi, gf, gft, c0, n0, m0, u, conv_w, conv_b, ln_g, ln_b):
    L = CHUNK
    nc = GROUP_TOKENS // L
    n_ctx = GROUP_TOKENS // CTX_SEQ
    assert CTX_SEQ == L and n_ctx == nc
    per = L // HALO
    nh = GROUP_TOKENS // HALO
    conv_consts = (conv_w, conv_b, ln_g, ln_b)
    fwd = lambda g, c: c
    bwd = lambda g, c: jnp.where(g == 0, c, nc - 1 - c)
    tok = lambda width, pos: pl.BlockSpec((None, L, width), lambda g, c: (g, pos(g, c), 0))
    feat = lambda rows, pos: pl.BlockSpec((None, rows, L), lambda g, c: (g, 0, pos(g, c)))
    lat = lambda g, c: jnp.maximum(g - 1, 0)
    ctx = lambda g, c: jnp.where(g == 0, c, n_ctx - 1)
    per_dir = lambda pos: [feat(M_WIDTH, pos), tok(M_WIDTH, pos), feat(M_WIDTH, pos),
                           tok(128, pos), tok(128, pos), feat(N_STREAMS, pos)]
    return pl.pallas_call(
        _mlstm_kernel,
        grid=(N_GROUPS, nc),
        in_specs=per_dir(fwd) + per_dir(bwd) + [
            pl.BlockSpec((None, N_STREAMS, M_DK, M_DK), lambda g, c: (lat(g, c), 0, 0, 0)),
            pl.BlockSpec((None, N_STREAMS, M_DK), lambda g, c: (lat(g, c), 0, 0)),
            pl.BlockSpec((None, 1, 128), lambda g, c: (lat(g, c), 0, 0)),
            pl.BlockSpec((None, HALO, CONV_CH),
                         lambda g, c: (g, jnp.maximum(c * per - 1, 0), 0)),
            tok(CONV_CH, fwd),
            pl.BlockSpec((None, HALO, CONV_CH),
                         lambda g, c: (g, jnp.minimum((c + 1) * per, nh - 1), 0)),
        ] + [_resident_spec(a) for a in conv_consts],
        out_specs=[
            tok(M_WIDTH, fwd), tok(M_WIDTH, bwd),
            pl.BlockSpec((None, N_STREAMS, M_DK, M_DK), lambda g, c: (ctx(g, c), 0, 0, 0)),
            pl.BlockSpec((None, N_STREAMS, M_DK), lambda g, c: (ctx(g, c), 0, 0)),
            pl.BlockSpec((None, 1, 128), lambda g, c: (ctx(g, c), 0, 0)),
            tok(CONV_CH, fwd),
        ],
        out_shape=[
            jax.ShapeDtypeStruct((N_GROUPS, GROUP_TOKENS, M_WIDTH), F32),
            jax.ShapeDtypeStruct((N_GROUPS, GROUP_TOKENS, M_WIDTH), F32),
            jax.ShapeDtypeStruct((n_ctx, N_STREAMS, M_DK, M_DK), F32),
            jax.ShapeDtypeStruct((n_ctx, N_STREAMS, M_DK), F32),
            jax.ShapeDtypeStruct((n_ctx, 1, 128), F32),
            jax.ShapeDtypeStruct((N_GROUPS, GROUP_TOKENS, CONV_CH), BF16),
        ],
        scratch_shapes=[
            pltpu.VMEM((N_STREAMS, STATE_ROWS, M_DK), F32),
            pltpu.VMEM((1, 128), F32),
            pltpu.VMEM((SUBLANES, L + 2 * HALO, CONV_CH), F32),
            pltpu.VMEM((L, CONV_CH), F32),
        ],
        compiler_params=_ARB2,
        name="mlstm_scan",
    )(qt, k, vt, gi, gf, gft, qt, k, vt, gi, gf, gft, c0, n0, m0, u, u, u, *conv_consts)


def _conv_stage(g, t, nt, up_ref, uc_ref, un_ref, cw_ref, cb_ref, lg_ref, lb_ref,
                ext_ref, conv_ref):
    T = uc_ref.shape[0]
    prev_on = jnp.where((g > 0) & (t > 0), 1.0, 0.0).astype(F32)
    next_on = jnp.where((g > 0) & (t < nt - 1), 1.0, 0.0).astype(F32)
    ext_ref[0, 0:HALO, :] = up_ref[...] * prev_on
    ext_ref[0, HALO:HALO + T, :] = uc_ref[...]
    ext_ref[0, HALO + T:HALO + T + HALO, :] = un_ref[...] * next_on
    span = T + 2 * HALO - SUBLANES
    for s in range(1, SUBLANES):
        ext_ref[s, 0:span, :] = ext_ref[0, s:s + span, :]

    groups = 4
    off = HALO - CONV_WIDTH // 2
    for r0 in range(0, T, groups * SUBLANES):
        accs = [cb_ref[...]] * groups
        for j in range(CONV_WIDTH):
            phase, base = (off + j) % SUBLANES, (off + j) // SUBLANES * SUBLANES
            w = cw_ref[j]
            for q in range(groups):
                lo = r0 + base + q * SUBLANES
                accs[q] = accs[q] + w * ext_ref[phase, lo:lo + SUBLANES, :]
        for q in range(groups):
            conv_ref[r0 + q * SUBLANES:r0 + (q + 1) * SUBLANES, :] = accs[q]

    y = conv_ref[...]
    mu = jnp.mean(y, axis=-1, keepdims=True)
    yc = y - mu
    var = jnp.mean(yc * yc, axis=-1, keepdims=True)
    y = yc * lax.rsqrt(var + EPS) * lg_ref[...] + lb_ref[...]
    return (y * _sigmoid(y)).astype(BF16)


def _mix_stage(hf_ref, hb_ref, o_ref, mg_ref, rows):
    hm = hf_ref[rows, :] + hb_ref[rows, :]
    hm = jnp.concatenate(
        [_rms(hm[:, h * M_DK:(h + 1) * M_DK]) for h in range(M_HEADS)], axis=-1)
    return (_sigmoid(o_ref[rows, :]) * (hm * mg_ref[...])).astype(BF16)


def _ffn_tail(x, mod_ref, g_ref, win_ref, wout_ref):
    hb = _modulated(x, g_ref[...], mod_ref[3:4, :], mod_ref[4:5, :]).astype(BF16)
    acc = jnp.zeros(x.shape, F32)
    for lo, hi in zip(FFN_CUTS[:-1], FFN_CUTS[1:]):
        gt = _dot(hb, win_ref[:, lo:hi])
        up = _dot(hb, win_ref[:, D_FF + lo:D_FF + hi])
        act = (gt * _sigmoid(gt) * up).astype(BF16)
        acc = acc + _dot(act, wout_ref[lo:hi, :])
    return x + mod_ref[5:6, :] * acc


def _l0_tail_kernel(u_ref, hf_ref, hb_ref, o_ref, xp_ref, xs_ref, mod_ref,
                    mg_ref, wu_ref, wm_ref, g2_ref, win_ref, wout_ref, out_ref):
    is_ctx = pl.program_id(0) == 0
    for r0 in range(0, u_ref.shape[0], FFN_CHAIN_ROWS):
        rows = slice(r0, r0 + FFN_CHAIN_ROWS)
        hm = _mix_stage(hf_ref, hb_ref, o_ref, mg_ref, rows)
        x = jnp.where(is_ctx, xp_ref[rows, :], xs_ref[rows, :])
        out = _dot(u_ref[rows, :], wu_ref[...]) + _dot(hm, wm_ref[...])
        out_ref[rows, :] = _ffn_tail(x + mod_ref[2:3, :] * out, mod_ref, g2_ref,
                                     win_ref, wout_ref)


def _l0_tail(u_conv, hf, hb, o, xp, xs, mod, mnorm_g, w_u, w_m, gain2, w_in, w_out):
    T = TOKEN_TILE
    nt = GROUP_TOKENS // T
    tok = lambda width: pl.BlockSpec((None, T, width), lambda g, t: (g, t, 0))
    consts = (mnorm_g, w_u, w_m, gain2, w_in, w_out)
    return pl.pallas_call(
        _l0_tail_kernel,
        grid=(N_GROUPS, nt),
        in_specs=[
            tok(CONV_CH), tok(M_WIDTH), tok(M_WIDTH), tok(M_WIDTH),
            _ctx_spec(T, nt, D_MODEL), _lat_spec(T, D_MODEL),
            pl.BlockSpec((None, 6, D_MODEL), lambda g, t: (g, 0, 0)),
        ] + [_resident_spec(a) for a in consts],
        out_specs=tok(D_MODEL),
        out_shape=jax.ShapeDtypeStruct((N_GROUPS, GROUP_TOKENS, D_MODEL), F32),
        compiler_params=_ARB2,
        name="l0_mix_ffn",
    )(u_conv, hf, hb, o, xp, xs, mod, *consts)


def _l1_tail_kernel(o_ref, x_ref, mod_ref, wc_ref, g2_ref, win_ref, wout_ref, fg_ref,
                    yp_ref, ys_ref):
    x = x_ref[...] + mod_ref[2:3, :] * _dot(o_ref[...], wc_ref[...])
    y = _rms(_ffn_tail(x, mod_ref, g2_ref, win_ref, wout_ref)) * fg_ref[...]
    g = pl.program_id(0)

    @pl.when(g == 0)
    def _():
        yp_ref[...] = y

    @pl.when(g > 0)
    def _():
        ys_ref[...] = y


def _l1_tail(o, x, mod, w_c, gain2, w_in, w_out, final_gain):
    tm = TOKEN_TILE
    nt = GROUP_TOKENS // tm
    tok = pl.BlockSpec((None, tm, D_MODEL), lambda g, t: (g, t, 0))
    consts = (w_c, gain2, w_in, w_out, final_gain)
    return pl.pallas_call(
        _l1_tail_kernel,
        grid=(N_GROUPS, nt),
        in_specs=[tok, tok, pl.BlockSpec((None, 6, D_MODEL), lambda g, t: (g, 0, 0))]
                 + [_resident_spec(a) for a in consts],
        out_specs=[_ctx_spec(tm, nt, D_MODEL), _lat_spec(tm, D_MODEL)],
        out_shape=[jax.ShapeDtypeStruct((1, GROUP_TOKENS, D_MODEL), F32),
                   jax.ShapeDtypeStruct((N_GROUPS - 1, GROUP_TOKENS, D_MODEL), F32)],
        compiler_params=_ARB2,
        name="l1_proj_ffn",
    )(o, x, mod, *consts)


def _rope(x, cos, sin_lo, sin_hi):
    quarter = HEAD_DIM // 4
    return (x * cos + pltpu.roll(x, HEAD_DIM - quarter, 1) * sin_lo
            + pltpu.roll(x, quarter, 1) * sin_hi)


def _c_in_kernel(x_ref, mod_ref, g_ref, w_ref, qg_ref, kg_ref, cos_ref, slo_ref, shi_ref,
                 q_ref, k_ref, v_ref, kn_ref, vn_ref):
    hb = _modulated(x_ref[...], g_ref[...], mod_ref[0:1, :], mod_ref[1:2, :]).astype(BF16)
    latent = pl.program_id(0) > 0
    cos = jnp.where(latent, cos_ref[...], 1.0)
    slo = jnp.where(latent, slo_ref[...], 0.0)
    shi = jnp.where(latent, shi_ref[...], 0.0)
    q = _dot(hb, w_ref[:, 0:N_Q * HEAD_DIM])
    qg = qg_ref[...] * (HEAD_DIM ** -0.5 * LOG2_E)
    for h in range(N_Q):
        lanes = slice(h * HEAD_DIM, (h + 1) * HEAD_DIM)
        q_ref[:, lanes] = _rope(_rms(q[:, lanes]) * qg, cos, slo, shi).astype(BF16)
    kv = _dot(hb, w_ref[:, N_Q * HEAD_DIM:(N_Q + 2 * N_KV) * HEAD_DIM])
    kg = kg_ref[...]
    kn = [_rms(kv[:, h * HEAD_DIM:(h + 1) * HEAD_DIM]) * kg for h in range(N_KV)]
    for h in range(N_KV):
        k_ref[:, h * HEAD_DIM:(h + 1) * HEAD_DIM] = _rope(kn[h], cos, slo, shi).astype(BF16)
    v = kv[:, N_KV * HEAD_DIM:]
    v_ref[...] = v.astype(BF16)

    @pl.when(pl.program_id(0) == 0)
    def _():
        for h in range(N_KV):
            kn_ref[:, h * HEAD_DIM:(h + 1) * HEAD_DIM] = kn[h]
        vn_ref[...] = v


def _c_in(x, mod, gain, w, qg, kg, cos, sin_lo, sin_hi):
    tm = ROPE_TILE
    kvw = N_KV * HEAD_DIM
    tok = lambda width: pl.BlockSpec((None, tm, width), lambda g, t: (g, t, 0))
    const = _resident_spec
    table = pl.BlockSpec((tm, HEAD_DIM), lambda g, t: (t, 0))
    nt = GROUP_TOKENS // tm
    ctx_only = pl.BlockSpec((tm, kvw), lambda g, t: (jnp.where(g == 0, t, nt - 1), 0))
    sds = lambda width, dt: jax.ShapeDtypeStruct((N_GROUPS, GROUP_TOKENS, width), dt)
    return pl.pallas_call(
        _c_in_kernel,
        grid=(N_GROUPS, GROUP_TOKENS // tm),
        in_specs=[tok(D_MODEL), pl.BlockSpec((None, 6, D_MODEL), lambda g, t: (g, 0, 0)),
                  const(gain), const(w), const(qg), const(kg), table, table, table],
        out_specs=[tok(N_Q * HEAD_DIM), tok(kvw), tok(kvw), ctx_only, ctx_only],
        out_shape=[sds(N_Q * HEAD_DIM, BF16), sds(kvw, BF16), sds(kvw, BF16),
                   jax.ShapeDtypeStruct((GROUP_TOKENS, kvw), F32),
                   jax.ShapeDtypeStruct((GROUP_TOKENS, kvw), F32)],
        compiler_params=_ARB2,
        name="attn_in_proj",
    )(x, mod, gain, w, qg, kg, cos, sin_lo, sin_hi)


def _group_softmax_pv(q_ref, o_ref, rows, k, v_aug):
    for i in range(GROUP):
        lanes = slice(i * HEAD_DIM, (i + 1) * HEAD_DIM)
        s = _dot_nt(q_ref[rows, lanes], k)
        p = jnp.exp2(s - jnp.max(s, axis=-1, keepdims=True)).astype(BF16)
        o = _dot(p, v_aug)
        o_ref[rows, lanes] = (o[:, :HEAD_DIM] / o[:, HEAD_DIM:HEAD_DIM + 1]).astype(BF16)


def _ones_column(rows):
    lane = lax.broadcasted_iota(jnp.int32, (rows, HEAD_DIM), 1)
    return jnp.where(lane == 0, 1.0, 0.0).astype(BF16)


def _attn_kernel(q_ref, k_ref, v_ref, kc_ref, vc_ref, o_ref, kall_ref, vall_ref):
    g = pl.program_id(0)
    t = pl.program_id(2)
    T = q_ref.shape[0]
    S = k_ref.shape[0]

    @pl.when(g == 0)
    def _():
        for j in range(T // CTX_SEQ):
            keys = pl.ds(pl.multiple_of(t * T + j * CTX_SEQ, CTX_SEQ), CTX_SEQ)
            v_aug = jnp.concatenate([v_ref[keys, :], _ones_column(CTX_SEQ)], axis=-1)
            _group_softmax_pv(q_ref, o_ref, slice(j * CTX_SEQ, (j + 1) * CTX_SEQ),
                              k_ref[keys, :], v_aug)

    @pl.when((g > 0) & (t == 0))
    def _():
        kall_ref[0:S, :] = k_ref[...]
        kall_ref[S:, :] = kc_ref[...]
        vall_ref[0:S, 0:HEAD_DIM] = v_ref[...]
        vall_ref[S:, 0:HEAD_DIM] = vc_ref[...]
        vall_ref[:, HEAD_DIM:] = _ones_column(vall_ref.shape[0])

    @pl.when(g > 0)
    def _():
        for r0 in range(0, T, ATTN_CHAIN_ROWS):
            _group_softmax_pv(q_ref, o_ref, slice(r0, r0 + ATTN_CHAIN_ROWS),
                              kall_ref[...], vall_ref[...])


def _attention(q, k, v, cache_k, cache_v):
    T = ATTN_TILE
    assert T % CTX_SEQ == 0
    gw = GROUP * HEAD_DIM
    qspec = pl.BlockSpec((None, T, gw), lambda g, h, t: (g, t, h))
    kspec = pl.BlockSpec((None, GROUP_TOKENS, HEAD_DIM), lambda g, h, t: (g, 0, h))
    cspec = pl.BlockSpec((None, PAST_LEN, HEAD_DIM),
                         lambda g, h, t: (jnp.maximum(g - 1, 0), 0, h))
    n_keys = GROUP_TOKENS + PAST_LEN
    return pl.pallas_call(
        _attn_kernel,
        grid=(N_GROUPS, N_KV, GROUP_TOKENS // T),
        in_specs=[qspec, kspec, kspec, cspec, cspec],
        out_specs=qspec,
        out_shape=jax.ShapeDtypeStruct(q.shape, BF16),
        scratch_shapes=[pltpu.VMEM((n_keys, HEAD_DIM), BF16),
                        pltpu.VMEM((n_keys, 2 * HEAD_DIM), BF16)],
        compiler_params=_ARB3,
        name="gqa_attention",
    )(q, k, v, cache_k, cache_v)


def _rope_tables():
    rows = GROUP_TOKENS // GRID_W
    assert rows == GRID_W
    n_freq = HEAD_DIM // 4
    inv = 1.0 / (ROPE_THETA ** (jnp.arange(n_freq, dtype=F32) / n_freq))
    ang = jnp.arange(GRID_W).astype(F32)[:, None] * inv
    by_row = lambda a: jnp.repeat(a, GRID_W, axis=0)
    by_col = lambda a: jnp.tile(a, (rows, 1))
    cos_r, cos_c = by_row(jnp.cos(ang)), by_col(jnp.cos(ang))
    sin_r, sin_c = by_row(jnp.sin(ang)), by_col(jnp.sin(ang))
    zero = jnp.zeros_like(cos_r)
    cos = jnp.concatenate([cos_r, cos_r, cos_c, cos_c], axis=-1)
    sin_lo = jnp.concatenate([-sin_r, zero, -sin_c, zero], axis=-1)
    sin_hi = jnp.concatenate([zero, sin_r, zero, sin_c], axis=-1)
    return cos, sin_lo, sin_hi


def kernel(x_prompt, x_sample, c, state_mlstm_C, state_mlstm_n, state_mlstm_m, cache_k, cache_v,
           c_ctx, w_mod, b_mod, norm1_g, norm2_g, w_in_ab, b_gate_ab, conv_w, conv_b, conv_ln_g,
           conv_ln_b, mlstm_norm_g, w_out_ab, w_in_c, q_norm_g, k_norm_g, w_out_c, w_ffn_in,
           w_ffn_out, final_norm_g):
    n_lat = x_sample.shape[0]
    n_ctx = x_prompt.shape[0]
    xp = x_prompt.reshape(1, GROUP_TOKENS, D_MODEL)

    cond8 = jnp.concatenate([c_ctx[None, :], c, jnp.zeros((8 - 1 - n_lat, D_MODEL), F32)], axis=0)
    mods = _adaln_all(cond8, w_mod, b_mod)[:, :N_GROUPS].reshape(2, N_GROUPS, 6, D_MODEL)
    row = lambda a: a.reshape(1, -1)

    main_cols = 2 * CONV_CH + 4 * M_WIDTH
    w_ab = w_in_ab[0]
    w_main = w_ab[:, :main_cols].astype(BF16)
    q_cols = slice(2 * CONV_CH, 2 * CONV_CH + M_WIDTH)
    v_cols = slice(2 * CONV_CH + 2 * M_WIDTH, 2 * CONV_CH + 3 * M_WIDTH)
    wqvt = jnp.concatenate([w_ab[:, q_cols], w_ab[:, v_cols]], axis=1).T.astype(BF16)
    gate_w = w_ab[:, main_cols:].reshape(D_MODEL, 2, 2, M_HEADS)
    gate_b = b_gate_ab[0].reshape(2, 2, M_HEADS)
    pad = 128 - N_STREAMS
    lanes = lambda a: jnp.pad(a.reshape(-1, N_STREAMS), ((0, 0), (0, pad)))
    wg = jnp.concatenate([lanes(gate_w[:, :, 0]), lanes(gate_w[:, :, 1])], axis=1).astype(BF16)
    bg = jnp.concatenate([lanes(gate_b[:, 0]), lanes(gate_b[:, 1])], axis=1)
    wgft = gate_w[:, :, 1].reshape(D_MODEL, N_STREAMS).T.astype(BF16)
    bgft = gate_b[:, 1].reshape(N_STREAMS, 1)
    u, qt, k, vt, o, gi, gf, gft = _ab_in(xp, x_sample, mods[0], row(norm1_g[0]), w_main, wqvt,
                                          wg, wgft, bg, bgft)

    c0 = state_mlstm_C[:, 0].reshape(n_lat, N_STREAMS, M_DK, M_DK)
    n0 = state_mlstm_n[:, 0].reshape(n_lat, N_STREAMS, M_DK)
    m0 = jnp.pad(state_mlstm_m[:, 0].reshape(n_lat, N_STREAMS), ((0, 0), (0, pad)))
    conv_w_rep = jnp.broadcast_to(conv_w[0][:, None, :], (CONV_WIDTH, SUBLANES, CONV_CH))
    conv_b_rep = jnp.broadcast_to(conv_b[0][None, :], (SUBLANES, CONV_CH))
    hf, hb, new_c, new_n, new_m, u_conv = _mlstm(
        qt, k, vt, gi, gf, gft, c0, n0, m0.reshape(n_lat, 1, 128),
        u, conv_w_rep, conv_b_rep, row(conv_ln_g[0]), row(conv_ln_b[0]))

    w_o = w_out_ab[0].astype(BF16)
    x = _l0_tail(u_conv, hf, hb, o, xp, x_sample, mods[0], row(mlstm_norm_g[0]),
                 w_o[:CONV_CH], w_o[CONV_CH:], row(norm2_g[0]),
                 w_ffn_in[0].astype(BF16), w_ffn_out[0].astype(BF16))

    cos, sin_lo, sin_hi = _rope_tables()
    qa, ka, va, k_norm, v_raw = _c_in(x, mods[1], row(norm1_g[1]), w_in_c[0].astype(BF16),
                                      row(q_norm_g[0]), row(k_norm_g[0]), cos, sin_lo, sin_hi)
    kvw = N_KV * HEAD_DIM
    ck = cache_k[:, 0].reshape(n_lat, PAST_LEN, kvw).astype(BF16)
    cv = cache_v[:, 0].reshape(n_lat, PAST_LEN, kvw).astype(BF16)
    oa = _attention(qa, ka, va, ck, cv)
    yp, y_sample = _l1_tail(oa, x, mods[1], w_out_c[0].astype(BF16), row(norm2_g[1]),
                            w_ffn_in[1].astype(BF16), w_ffn_out[1].astype(BF16),
                            row(final_norm_g))

    y_prompt = yp.reshape(n_ctx, CTX_SEQ, D_MODEL)
    new_state_c = new_c.reshape(n_ctx, 1, 2, M_HEADS, M_DK, M_DK)
    new_state_n = new_n.reshape(n_ctx, 1, 2, M_HEADS, M_DK)
    new_state_m = new_m[:, 0, :N_STREAMS].reshape(n_ctx, 1, 2, M_HEADS)
    new_k = k_norm.reshape(n_ctx, 1, CTX_SEQ, N_KV, HEAD_DIM)
    new_v = v_raw.reshape(n_ctx, 1, CTX_SEQ, N_KV, HEAD_DIM)
    return (y_prompt, y_sample, new_state_c, new_state_n, new_state_m, new_k, new_v)
```

```python
import jax
import jax.numpy as jnp
from jax import lax
from jax.experimental import pallas as pl
from jax.experimental.pallas import tpu as pltpu

F32 = jnp.float32
BF16 = jnp.bfloat16

D_MODEL = 1024
N_GROUPS = 5
GROUP_TOKENS = 4096
CTX_SEQ = 256
CONV_CH = 512
CONV_WIDTH = 31
M_HEADS = 4
M_DK = 128
M_WIDTH = 512
N_STREAMS = 2 * M_HEADS
STATE_ROWS = M_DK + 16
HEAD_DIM = 128
N_Q = 8
N_KV = 2
GROUP = N_Q // N_KV
PAST_LEN = 256
GRID_W = 64
ROPE_THETA = 10000.0
D_FF = 2816
EPS = 1e-6
LOG2_E = 1.4426950408889634

CHUNK = 256
TOKEN_TILE = 512
ROPE_TILE = 256
ATTN_TILE = 512
ATTN_CHAIN_ROWS = 128
FFN_CUTS = (0, 1536, D_FF)
FFN_CHAIN_ROWS = 512
HALO = 16
SUBLANES = 8
VMEM_LIMIT = 56 * 1024 * 1024

_ARB2 = pltpu.CompilerParams(dimension_semantics=("arbitrary", "arbitrary"),
                             vmem_limit_bytes=VMEM_LIMIT)
_ARB3 = pltpu.CompilerParams(dimension_semantics=("arbitrary", "arbitrary", "arbitrary"),
                             vmem_limit_bytes=VMEM_LIMIT)


def _rms(x):
    return x * lax.rsqrt(jnp.mean(x * x, axis=-1, keepdims=True) + EPS)


def _modulated(x, gain, shift, scale):
    return (_rms(x) * gain) * (1.0 + scale) + shift


def _sigmoid(x):
    return 1.0 / (1.0 + jnp.exp(-x))


def _log_sigmoid(x):
    return jnp.minimum(x, 0.0) - jnp.log1p(jnp.exp(-jnp.abs(x)))


def _dot(a, b):
    return jnp.dot(a, b, preferred_element_type=F32)


def _dot_nt(a, b):
    return lax.dot_general(a, b, (((1,), (1,)), ((), ())), preferred_element_type=F32)


def _prefix_sum(x, axis):
    pos = lax.broadcasted_iota(jnp.int32, x.shape, axis)
    k = 1
    while k < x.shape[axis]:
        x = x + jnp.where(pos >= k, pltpu.roll(x, k, axis), 0.0)
        k *= 2
    return x


def _resident_spec(a):
    return pl.BlockSpec(a.shape, lambda *_: (0,) * a.ndim, pipeline_mode=pl.Buffered(1))


def _ctx_spec(tile, nt, width):
    return pl.BlockSpec((None, tile, width), lambda g, t: (0, jnp.where(g == 0, t, nt - 1), 0))


def _lat_spec(tile, width):
    return pl.BlockSpec((None, tile, width),
                        lambda g, t: (jnp.maximum(g - 1, 0), jnp.where(g > 0, t, 0), 0))


def _mod_kernel(cond_ref, w_ref, b_ref, o_ref):
    c = cond_ref[...]
    s = (c * _sigmoid(c)).astype(BF16)
    o_ref[...] = _dot(s, w_ref[...].astype(BF16)) + b_ref[...]


def _adaln_all(cond8, w_mod, b_mod):
    depth, d, n = w_mod.shape
    tn = 1024
    return pl.pallas_call(
        _mod_kernel,
        grid=(depth, n // tn),
        in_specs=[
            pl.BlockSpec((8, d), lambda l, j: (0, 0)),
            pl.BlockSpec((None, d, tn), lambda l, j: (l, 0, j)),
            pl.BlockSpec((None, 1, tn), lambda l, j: (l, 0, j)),
        ],
        out_specs=pl.BlockSpec((None, 8, tn), lambda l, j: (l, 0, j)),
        out_shape=jax.ShapeDtypeStruct((depth, 8, n), F32),
        compiler_params=_ARB2,
        name="adaln_mod",
    )(cond8, w_mod, b_mod.reshape(depth, 1, n))


def _ab_in_kernel(xp_ref, xs_ref, mod_ref, g_ref, w_ref, wqvt_ref,
                  wg_ref, wgft_ref, bg_ref, bgft_ref,
                  u_ref, qt_ref, k_ref, vt_ref, o_ref, gi_ref, gf_ref, gft_ref):
    x = jnp.where(pl.program_id(0) == 0, xp_ref[...], xs_ref[...])
    h = _modulated(x, g_ref[...], mod_ref[0:1, :], mod_ref[1:2, :])
    hb = h.astype(BF16)

    def proj(c):
        return _dot(hb, w_ref[:, c * CONV_CH:(c + 1) * CONV_CH])

    u_ref[...] = proj(0) * _sigmoid(proj(1))
    k_ref[...] = proj(3) * (M_DK ** -0.5)
    o_ref[...] = proj(5)
    qvt = _dot_nt(wqvt_ref[...], hb).astype(BF16)
    qt_ref[...] = qvt[:M_WIDTH]
    vt_ref[...] = qvt[M_WIDTH:]
    lane = lax.broadcasted_iota(jnp.int32, (x.shape[0], 128), 1)
    gates = _dot(hb, wg_ref[...]) + bg_ref[...]
    gi_ref[...] = gates[:, :128]
    gf_ref[...] = jnp.where(lane < N_STREAMS, _log_sigmoid(gates[:, 128:]), 0.0)
    gft_ref[...] = _log_sigmoid(_dot_nt(wgft_ref[...], hb) + bgft_ref[...])


def _ab_in(xp, xs, mod, gain, w_main, wqvt, wg, wgft, bg, bgft):
    tm = TOKEN_TILE
    nt = GROUP_TOKENS // tm
    tok = lambda width: pl.BlockSpec((None, tm, width), lambda g, t: (g, t, 0))
    feat = lambda rows: pl.BlockSpec((None, rows, tm), lambda g, t: (g, 0, t))
    const = _resident_spec
    tok_sds = lambda width: jax.ShapeDtypeStruct((N_GROUPS, GROUP_TOKENS, width), F32)
    feat_sds = lambda rows, dt: jax.ShapeDtypeStruct((N_GROUPS, rows, GROUP_TOKENS), dt)
    consts = (gain, w_main, wqvt, wg, wgft, bg, bgft)
    return pl.pallas_call(
        _ab_in_kernel,
        grid=(N_GROUPS, nt),
        in_specs=[_ctx_spec(tm, nt, D_MODEL), _lat_spec(tm, D_MODEL),
                  pl.BlockSpec((None, 6, D_MODEL), lambda g, t: (g, 0, 0))]
                 + [const(a) for a in consts],
        out_specs=[tok(CONV_CH), feat(M_WIDTH), tok(M_WIDTH), feat(M_WIDTH), tok(M_WIDTH),
                   tok(128), tok(128), feat(N_STREAMS)],
        out_shape=[tok_sds(CONV_CH), feat_sds(M_WIDTH, BF16), tok_sds(M_WIDTH),
                   feat_sds(M_WIDTH, BF16), tok_sds(M_WIDTH),
                   tok_sds(128), tok_sds(128), feat_sds(N_STREAMS, F32)],
        compiler_params=_ARB2,
        name="ab_in_proj",
    )(xp, xs, mod, *consts)


def _mlstm_direction(qt_ref, k_ref, vt_ref, gi_ref, gf_ref, gft_ref, h_ref, cn_s, m_s, d,
                     side_jobs):
    L = k_ref.shape[0]
    r = lax.broadcasted_iota(jnp.int32, (L, L), 0)
    c = lax.broadcasted_iota(jnp.int32, (L, L), 1)
    le = r <= c
    ge = r >= c
    gf = gf_ref[...]
    gft = gft_ref[...]
    p_col = _prefix_sum(gf, 0)
    p_row = _prefix_sum(gft, 1)
    tot_col = p_col[L - 1:L, :]
    if d == 0:
        b_col, b_row, mask = p_col, p_row, le
    else:
        b_col = tot_col - p_col + gf
        b_row = p_row[:, L - 1:L] - p_row + gft
        mask = ge
    c_all = gi_ref[...] - b_col
    m_prev_all = m_s[...]
    m_fin = jnp.maximum(m_prev_all, jnp.max(c_all, axis=0, keepdims=True))
    w_all = jnp.exp(c_all - m_fin)
    a_all = jnp.exp(m_prev_all - m_fin)
    lane = lax.broadcasted_iota(jnp.int32, (1, 128), 1)
    mine = (lane >= d * M_HEADS) & (lane < (d + 1) * M_HEADS)
    m_s[...] = jnp.where(mine, tot_col + m_fin, m_prev_all)
    ones_row = jnp.where(lax.broadcasted_iota(jnp.int32, (STATE_ROWS - M_DK, L), 0) == 0,
                         1.0, 0.0).astype(BF16)
    for h in range(M_HEADS):
        s_idx = d * M_HEADS + h
        sl = slice(h * M_DK, (h + 1) * M_DK)
        c_col = c_all[:, s_idx:s_idx + 1]
        m_prev = m_prev_all[:, s_idx:s_idx + 1]
        cm = jnp.where(mask, c_col, -jnp.inf)
        m_row = jnp.maximum(jnp.max(cm, axis=0, keepdims=True), m_prev)
        dmat = jnp.exp(cm - m_row)
        k = k_ref[:, sl]
        cn = cn_s[s_idx]
        lhs = jnp.concatenate([k.astype(BF16), cn.astype(BF16)], axis=0)
        r1 = _dot(lhs, qt_ref[sl, :])
        st = (r1[0:L] * dmat).astype(BF16)
        vt_aug = jnp.concatenate([vt_ref[sl, :], ones_row], axis=0)
        w_inter = jnp.exp(m_prev - m_row)
        tot = _dot(vt_aug, st) + w_inter * r1[L:L + STATE_ROWS]
        den = tot[M_DK:M_DK + 1, :]
        m_t = b_row[s_idx:s_idx + 1, :] + m_row
        ht = tot[0:M_DK, :] / jnp.maximum(jnp.abs(den), jnp.exp(-m_t))
        h_ref[:, sl] = ht.T
        kw = (k * w_all[:, s_idx:s_idx + 1]).astype(BF16)
        cn_s[s_idx] = a_all[:, s_idx:s_idx + 1] * cn + _dot(vt_aug, kw)
        side_jobs.pop(0)()


def _mlstm_kernel(qtf_ref, kf_ref, vtf_ref, gif_ref, gff_ref, gftf_ref,
                  qtb_ref, kb_ref, vtb_ref, gib_ref, gfb_ref, gftb_ref,
                  c0_ref, n0_ref, m0_ref,
                  up_ref, uc_ref, un_ref, cw_ref, cb_ref, lg_ref, lb_ref,
                  hf_ref, hb_ref, co_ref, no_ref, mo_ref, uconv_ref,
                  cn_s, m_s, ext_ref):
    g = pl.program_id(0)
    c = pl.program_id(1)


    @pl.when(g == 0)
    def _():
        cn_s[...] = jnp.zeros_like(cn_s)
        m_s[...] = jnp.zeros_like(m_s)

    @pl.when((g > 0) & (c == 0))
    def _():
        cn_s[...] = jnp.zeros_like(cn_s)
        for s in range(N_STREAMS):
            cn_s[s, 0:M_DK, :] = c0_ref[s].T
            cn_s[s, M_DK:M_DK + 1, :] = n0_ref[s:s + 1, :]
        m_s[...] = m0_ref[...]

    jobs = _conv_jobs(g, c, pl.num_programs(1), up_ref, uc_ref, un_ref, cw_ref, cb_ref,
                      lg_ref, lb_ref, ext_ref, uconv_ref, N_STREAMS)
    jobs.pop(0)()
    _mlstm_direction(qtf_ref, kf_ref, vtf_ref, gif_ref, gff_ref, gftf_ref, hf_ref, cn_s, m_s, 0,
                     jobs)
    _mlstm_direction(qtb_ref, kb_ref, vtb_ref, gib_ref, gfb_ref, gftb_ref, hb_ref, cn_s, m_s, 1,
                     jobs)

    @pl.when(g == 0)
    def _():
        for s in range(N_STREAMS):
            co_ref[s] = cn_s[s, 0:M_DK, :].T
            no_ref[s:s + 1, :] = cn_s[s, M_DK:M_DK + 1, :]
        mo_ref[...] = m_s[...]


def _mlstm(qt, k, vt, gi, gf, gft, c0, n0, m0, u, conv_w, conv_b, ln_g, ln_b):
    L = CHUNK
    nc = GROUP_TOKENS // L
    n_ctx = GROUP_TOKENS // CTX_SEQ
    assert CTX_SEQ == L and n_ctx == nc
    per = L // HALO
    nh = GROUP_TOKENS // HALO
    conv_consts = (conv_w, conv_b, ln_g, ln_b)
    fwd = lambda g, c: c
    bwd = lambda g, c: jnp.where(g == 0, c, nc - 1 - c)
    tok = lambda width, pos: pl.BlockSpec((None, L, width), lambda g, c: (g, pos(g, c), 0))
    feat = lambda rows, pos: pl.BlockSpec((None, rows, L), lambda g, c: (g, 0, pos(g, c)))
    lat = lambda g, c: jnp.maximum(g - 1, 0)
    ctx = lambda g, c: jnp.where(g == 0, c, n_ctx - 1)
    per_dir = lambda pos: [feat(M_WIDTH, pos), tok(M_WIDTH, pos), feat(M_WIDTH, pos),
                           tok(128, pos), tok(128, pos), feat(N_STREAMS, pos)]
    return pl.pallas_call(
        _mlstm_kernel,
        grid=(N_GROUPS, nc),
        in_specs=per_dir(fwd) + per_dir(bwd) + [
            pl.BlockSpec((None, N_STREAMS, M_DK, M_DK), lambda g, c: (lat(g, c), 0, 0, 0)),
            pl.BlockSpec((None, N_STREAMS, M_DK), lambda g, c: (lat(g, c), 0, 0)),
            pl.BlockSpec((None, 1, 128), lambda g, c: (lat(g, c), 0, 0)),
            pl.BlockSpec((None, HALO, CONV_CH),
                         lambda g, c: (g, jnp.maximum(c * per - 1, 0), 0)),
            tok(CONV_CH, fwd),
            pl.BlockSpec((None, HALO, CONV_CH),
                         lambda g, c: (g, jnp.minimum((c + 1) * per, nh - 1), 0)),
        ] + [_resident_spec(a) for a in conv_consts],
        out_specs=[
            tok(M_WIDTH, fwd), tok(M_WIDTH, bwd),
            pl.BlockSpec((None, N_STREAMS, M_DK, M_DK), lambda g, c: (ctx(g, c), 0, 0, 0)),
            pl.BlockSpec((None, N_STREAMS, M_DK), lambda g, c: (ctx(g, c), 0, 0)),
            pl.BlockSpec((None, 1, 128), lambda g, c: (ctx(g, c), 0, 0)),
            tok(CONV_CH, fwd),
        ],
        out_shape=[
            jax.ShapeDtypeStruct((N_GROUPS, GROUP_TOKENS, M_WIDTH), F32),
            jax.ShapeDtypeStruct((N_GROUPS, GROUP_TOKENS, M_WIDTH), F32),
            jax.ShapeDtypeStruct((n_ctx, N_STREAMS, M_DK, M_DK), F32),
            jax.ShapeDtypeStruct((n_ctx, N_STREAMS, M_DK), F32),
            jax.ShapeDtypeStruct((n_ctx, 1, 128), F32),
            jax.ShapeDtypeStruct((N_GROUPS, GROUP_TOKENS, CONV_CH), BF16),
        ],
        scratch_shapes=[
            pltpu.VMEM((N_STREAMS, STATE_ROWS, M_DK), F32),
            pltpu.VMEM((1, 128), F32),
            pltpu.VMEM((SUBLANES, L + 2 * HALO, CONV_CH), F32),
        ],
        compiler_params=_ARB2,
        name="mlstm_scan",
    )(qt, k, vt, gi, gf, gft, qt, k, vt, gi, gf, gft, c0, n0, m0, u, u, u, *conv_consts)


def _conv_jobs(g, t, nt, up_ref, uc_ref, un_ref, cw_ref, cb_ref, lg_ref, lb_ref,
               ext_ref, out_ref, n_blocks):
    T = uc_ref.shape[0]
    rows = T // n_blocks
    span = T + 2 * HALO - SUBLANES
    off = HALO - CONV_WIDTH // 2

    def prepare():
        prev_on = jnp.where((g > 0) & (t > 0), 1.0, 0.0).astype(F32)
        next_on = jnp.where((g > 0) & (t < nt - 1), 1.0, 0.0).astype(F32)
        ext_ref[0, 0:HALO, :] = up_ref[...] * prev_on
        ext_ref[0, HALO:HALO + T, :] = uc_ref[...]
        ext_ref[0, HALO + T:HALO + T + HALO, :] = un_ref[...] * next_on
        for s in range(1, SUBLANES):
            ext_ref[s, 0:span, :] = ext_ref[0, s:s + span, :]

    def block(r0):
        def run():
            groups = rows // SUBLANES
            accs = [cb_ref[...]] * groups
            for j in range(CONV_WIDTH):
                phase, base = (off + j) % SUBLANES, (off + j) // SUBLANES * SUBLANES
                w = cw_ref[j]
                for q in range(groups):
                    lo = r0 + base + q * SUBLANES
                    accs[q] = accs[q] + w * ext_ref[phase, lo:lo + SUBLANES, :]
            y = jnp.concatenate(accs, axis=0)
            mu = jnp.mean(y, axis=-1, keepdims=True)
            yc = y - mu
            var = jnp.mean(yc * yc, axis=-1, keepdims=True)
            y = yc * lax.rsqrt(var + EPS) * lg_ref[...] + lb_ref[...]
            out_ref[r0:r0 + rows, :] = (y * _sigmoid(y)).astype(BF16)
        return run

    return [prepare] + [block(r0) for r0 in range(0, T, rows)]


def _mix_stage(hf_ref, hb_ref, o_ref, mg_ref, rows):
    hm = hf_ref[rows, :] + hb_ref[rows, :]
    hm = jnp.concatenate(
        [_rms(hm[:, h * M_DK:(h + 1) * M_DK]) for h in range(M_HEADS)], axis=-1)
    return (_sigmoid(o_ref[rows, :]) * (hm * mg_ref[...])).astype(BF16)


def _ffn_tail(x, mod_ref, g_ref, win_ref, wout_ref):
    hb = _modulated(x, g_ref[...], mod_ref[3:4, :], mod_ref[4:5, :]).astype(BF16)
    acc = jnp.zeros(x.shape, F32)
    for lo, hi in zip(FFN_CUTS[:-1], FFN_CUTS[1:]):
        gt = _dot(hb, win_ref[:, lo:hi])
        up = _dot(hb, win_ref[:, D_FF + lo:D_FF + hi])
        act = (gt * _sigmoid(gt) * up).astype(BF16)
        acc = acc + _dot(act, wout_ref[lo:hi, :])
    return x + mod_ref[5:6, :] * acc


def _l0_tail_kernel(u_ref, hf_ref, hb_ref, o_ref, xp_ref, xs_ref, mod_ref,
                    mg_ref, wu_ref, wm_ref, g2_ref, win_ref, wout_ref, out_ref):
    is_ctx = pl.program_id(0) == 0
    for r0 in range(0, u_ref.shape[0], FFN_CHAIN_ROWS):
        rows = slice(r0, r0 + FFN_CHAIN_ROWS)
        hm = _mix_stage(hf_ref, hb_ref, o_ref, mg_ref, rows)
        x = jnp.where(is_ctx, xp_ref[rows, :], xs_ref[rows, :])
        out = _dot(u_ref[rows, :], wu_ref[...]) + _dot(hm, wm_ref[...])
        out_ref[rows, :] = _ffn_tail(x + mod_ref[2:3, :] * out, mod_ref, g2_ref,
                                     win_ref, wout_ref)


def _l0_tail(u_conv, hf, hb, o, xp, xs, mod, mnorm_g, w_u, w_m, gain2, w_in, w_out):
    T = TOKEN_TILE
    nt = GROUP_TOKENS // T
    tok = lambda width: pl.BlockSpec((None, T, width), lambda g, t: (g, t, 0))
    consts = (mnorm_g, w_u, w_m, gain2, w_in, w_out)
    return pl.pallas_call(
        _l0_tail_kernel,
        grid=(N_GROUPS, nt),
        in_specs=[
            tok(CONV_CH), tok(M_WIDTH), tok(M_WIDTH), tok(M_WIDTH),
            _ctx_spec(T, nt, D_MODEL), _lat_spec(T, D_MODEL),
            pl.BlockSpec((None, 6, D_MODEL), lambda g, t: (g, 0, 0)),
        ] + [_resident_spec(a) for a in consts],
        out_specs=tok(D_MODEL),
        out_shape=jax.ShapeDtypeStruct((N_GROUPS, GROUP_TOKENS, D_MODEL), F32),
        compiler_params=_ARB2,
        name="l0_mix_ffn",
    )(u_conv, hf, hb, o, xp, xs, mod, *consts)


def _l1_tail_kernel(o_ref, x_ref, mod_ref, wc_ref, g2_ref, win_ref, wout_ref, fg_ref,
                    yp_ref, ys_ref):
    x = x_ref[...] + mod_ref[2:3, :] * _dot(o_ref[...], wc_ref[...])
    y = _rms(_ffn_tail(x, mod_ref, g2_ref, win_ref, wout_ref)) * fg_ref[...]
    g = pl.program_id(0)

    @pl.when(g == 0)
    def _():
        yp_ref[...] = y

    @pl.when(g > 0)
    def _():
        ys_ref[...] = y


def _l1_tail(o, x, mod, w_c, gain2, w_in, w_out, final_gain):
    tm = TOKEN_TILE
    nt = GROUP_TOKENS // tm
    tok = pl.BlockSpec((None, tm, D_MODEL), lambda g, t: (g, t, 0))
    consts = (w_c, gain2, w_in, w_out, final_gain)
    return pl.pallas_call(
        _l1_tail_kernel,
        grid=(N_GROUPS, nt),
        in_specs=[tok, tok, pl.BlockSpec((None, 6, D_MODEL), lambda g, t: (g, 0, 0))]
                 + [_resident_spec(a) for a in consts],
        out_specs=[_ctx_spec(tm, nt, D_MODEL), _lat_spec(tm, D_MODEL)],
        out_shape=[jax.ShapeDtypeStruct((1, GROUP_TOKENS, D_MODEL), F32),
                   jax.ShapeDtypeStruct((N_GROUPS - 1, GROUP_TOKENS, D_MODEL), F32)],
        compiler_params=_ARB2,
        name="l1_proj_ffn",
    )(o, x, mod, *consts)


def _paired_layout(a):
    quarter = HEAD_DIM // 4
    blocks = a.reshape(a.shape[:-1] + (a.shape[-1] // HEAD_DIM, 4, quarter))
    return blocks[..., jnp.array([0, 2, 1, 3]), :].reshape(a.shape)


def _rope(x, cos, sin):
    return x * cos + pltpu.roll(x, HEAD_DIM // 2, 1) * sin


def _c_in_kernel(x_ref, mod_ref, g_ref, w_ref, qg_ref, kg_ref, cos_ref, sin_ref,
                 q_ref, k_ref, v_ref, kn_ref, vn_ref):
    hb = _modulated(x_ref[...], g_ref[...], mod_ref[0:1, :], mod_ref[1:2, :]).astype(BF16)
    latent = pl.program_id(0) > 0
    cos = jnp.where(latent, cos_ref[...], 1.0)
    sin = jnp.where(latent, sin_ref[...], 0.0)
    q = _dot(hb, w_ref[:, 0:N_Q * HEAD_DIM])
    qg = qg_ref[...] * (HEAD_DIM ** -0.5 * LOG2_E)
    for h in range(N_Q):
        lanes = slice(h * HEAD_DIM, (h + 1) * HEAD_DIM)
        q_ref[:, lanes] = _rope(_rms(q[:, lanes]) * qg, cos, sin).astype(BF16)
    kv = _dot(hb, w_ref[:, N_Q * HEAD_DIM:(N_Q + 2 * N_KV) * HEAD_DIM])
    kg = kg_ref[...]
    kn = [_rms(kv[:, h * HEAD_DIM:(h + 1) * HEAD_DIM]) * kg for h in range(N_KV)]
    for h in range(N_KV):
        k_ref[:, h * HEAD_DIM:(h + 1) * HEAD_DIM] = _rope(kn[h], cos, sin).astype(BF16)
    v = kv[:, N_KV * HEAD_DIM:]
    v_ref[...] = v.astype(BF16)

    @pl.when(pl.program_id(0) == 0)
    def _():
        for h in range(N_KV):
            kn_ref[:, h * HEAD_DIM:(h + 1) * HEAD_DIM] = kn[h]
        vn_ref[...] = v


def _c_in(x, mod, gain, w, qg, kg, cos, sin):
    tm = ROPE_TILE
    kvw = N_KV * HEAD_DIM
    tok = lambda width: pl.BlockSpec((None, tm, width), lambda g, t: (g, t, 0))
    const = _resident_spec
    table = pl.BlockSpec((tm, HEAD_DIM), lambda g, t: (t, 0))
    nt = GROUP_TOKENS // tm
    ctx_only = pl.BlockSpec((tm, kvw), lambda g, t: (jnp.where(g == 0, t, nt - 1), 0))
    sds = lambda width, dt: jax.ShapeDtypeStruct((N_GROUPS, GROUP_TOKENS, width), dt)
    return pl.pallas_call(
        _c_in_kernel,
        grid=(N_GROUPS, GROUP_TOKENS // tm),
        in_specs=[tok(D_MODEL), pl.BlockSpec((None, 6, D_MODEL), lambda g, t: (g, 0, 0)),
                  const(gain), const(w), const(qg), const(kg), table, table],
        out_specs=[tok(N_Q * HEAD_DIM), tok(kvw), tok(kvw), ctx_only, ctx_only],
        out_shape=[sds(N_Q * HEAD_DIM, BF16), sds(kvw, BF16), sds(kvw, BF16),
                   jax.ShapeDtypeStruct((GROUP_TOKENS, kvw), F32),
                   jax.ShapeDtypeStruct((GROUP_TOKENS, kvw), F32)],
        compiler_params=_ARB2,
        name="attn_in_proj",
    )(x, mod, gain, w, qg, kg, cos, sin)


def _group_softmax_pv(q_ref, o_ref, rows, k, v_aug):
    for i in range(GROUP):
        lanes = slice(i * HEAD_DIM, (i + 1) * HEAD_DIM)
        s = _dot_nt(q_ref[rows, lanes], k)
        p = jnp.exp2(s - jnp.max(s, axis=-1, keepdims=True)).astype(BF16)
        o = _dot(p, v_aug)
        o_ref[rows, lanes] = (o[:, :HEAD_DIM] / o[:, HEAD_DIM:HEAD_DIM + 1]).astype(BF16)


def _ones_column(rows):
    lane = lax.broadcasted_iota(jnp.int32, (rows, HEAD_DIM), 1)
    return jnp.where(lane == 0, 1.0, 0.0).astype(BF16)


def _attn_kernel(q_ref, k_ref, v_ref, kc_ref, vc_ref, o_ref, kall_ref, vall_ref):
    g = pl.program_id(0)
    t = pl.program_id(2)
    T = q_ref.shape[0]
    S = k_ref.shape[0]

    @pl.when(g == 0)
    def _():
        for j in range(T // CTX_SEQ):
            keys = pl.ds(pl.multiple_of(t * T + j * CTX_SEQ, CTX_SEQ), CTX_SEQ)
            v_aug = jnp.concatenate([v_ref[keys, :], _ones_column(CTX_SEQ)], axis=-1)
            _group_softmax_pv(q_ref, o_ref, slice(j * CTX_SEQ, (j + 1) * CTX_SEQ),
                              k_ref[keys, :], v_aug)

    @pl.when((g > 0) & (t == 0))
    def _():
        kall_ref[0:S, :] = k_ref[...]
        kall_ref[S:, :] = kc_ref[...]
        vall_ref[0:S, 0:HEAD_DIM] = v_ref[...]
        vall_ref[S:, 0:HEAD_DIM] = vc_ref[...]
        vall_ref[:, HEAD_DIM:] = _ones_column(vall_ref.shape[0])

    @pl.when(g > 0)
    def _():
        for r0 in range(0, T, ATTN_CHAIN_ROWS):
            _group_softmax_pv(q_ref, o_ref, slice(r0, r0 + ATTN_CHAIN_ROWS),
                              kall_ref[...], vall_ref[...])


def _attention(q, k, v, cache_k, cache_v):
    T = ATTN_TILE
    assert T % CTX_SEQ == 0
    gw = GROUP * HEAD_DIM
    qspec = pl.BlockSpec((None, T, gw), lambda g, h, t: (g, t, h))
    kspec = pl.BlockSpec((None, GROUP_TOKENS, HEAD_DIM), lambda g, h, t: (g, 0, h))
    cspec = pl.BlockSpec((None, PAST_LEN, HEAD_DIM),
                         lambda g, h, t: (jnp.maximum(g - 1, 0), 0, h))
    n_keys = GROUP_TOKENS + PAST_LEN
    return pl.pallas_call(
        _attn_kernel,
        grid=(N_GROUPS, N_KV, GROUP_TOKENS // T),
        in_specs=[qspec, kspec, kspec, cspec, cspec],
        out_specs=qspec,
        out_shape=jax.ShapeDtypeStruct(q.shape, BF16),
        scratch_shapes=[pltpu.VMEM((n_keys, HEAD_DIM), BF16),
                        pltpu.VMEM((n_keys, 2 * HEAD_DIM), BF16)],
        compiler_params=_ARB3,
        name="gqa_attention",
    )(q, k, v, cache_k, cache_v)


def _rope_tables():
    rows = GROUP_TOKENS // GRID_W
    assert rows == GRID_W
    n_freq = HEAD_DIM // 4
    inv = 1.0 / (ROPE_THETA ** (jnp.arange(n_freq, dtype=F32) / n_freq))
    ang = jnp.arange(GRID_W).astype(F32)[:, None] * inv
    by_row = lambda a: jnp.repeat(a, GRID_W, axis=0)
    by_col = lambda a: jnp.tile(a, (rows, 1))
    cos_r, cos_c = by_row(jnp.cos(ang)), by_col(jnp.cos(ang))
    sin_r, sin_c = by_row(jnp.sin(ang)), by_col(jnp.sin(ang))
    cos = jnp.concatenate([cos_r, cos_c, cos_r, cos_c], axis=-1)
    sin = jnp.concatenate([-sin_r, -sin_c, sin_r, sin_c], axis=-1)
    return cos, sin


def kernel(x_prompt, x_sample, c, state_mlstm_C, state_mlstm_n, state_mlstm_m, cache_k, cache_v,
           c_ctx, w_mod, b_mod, norm1_g, norm2_g, w_in_ab, b_gate_ab, conv_w, conv_b, conv_ln_g,
           conv_ln_b, mlstm_norm_g, w_out_ab, w_in_c, q_norm_g, k_norm_g, w_out_c, w_ffn_in,
           w_ffn_out, final_norm_g):
    n_lat = x_sample.shape[0]
    n_ctx = x_prompt.shape[0]
    xp = x_prompt.reshape(1, GROUP_TOKENS, D_MODEL)

    cond8 = jnp.concatenate([c_ctx[None, :], c, jnp.zeros((8 - 1 - n_lat, D_MODEL), F32)], axis=0)
    mods = _adaln_all(cond8, w_mod, b_mod)[:, :N_GROUPS].reshape(2, N_GROUPS, 6, D_MODEL)
    row = lambda a: a.reshape(1, -1)

    main_cols = 2 * CONV_CH + 4 * M_WIDTH
    w_ab = w_in_ab[0]
    w_main = w_ab[:, :main_cols].astype(BF16)
    q_cols = slice(2 * CONV_CH, 2 * CONV_CH + M_WIDTH)
    v_cols = slice(2 * CONV_CH + 2 * M_WIDTH, 2 * CONV_CH + 3 * M_WIDTH)
    wqvt = jnp.concatenate([w_ab[:, q_cols], w_ab[:, v_cols]], axis=1).T.astype(BF16)
    gate_w = w_ab[:, main_cols:].reshape(D_MODEL, 2, 2, M_HEADS)
    gate_b = b_gate_ab[0].reshape(2, 2, M_HEADS)
    pad = 128 - N_STREAMS
    lanes = lambda a: jnp.pad(a.reshape(-1, N_STREAMS), ((0, 0), (0, pad)))
    wg = jnp.concatenate([lanes(gate_w[:, :, 0]), lanes(gate_w[:, :, 1])], axis=1).astype(BF16)
    bg = jnp.concatenate([lanes(gate_b[:, 0]), lanes(gate_b[:, 1])], axis=1)
    wgft = gate_w[:, :, 1].reshape(D_MODEL, N_STREAMS).T.astype(BF16)
    bgft = gate_b[:, 1].reshape(N_STREAMS, 1)
    u, qt, k, vt, o, gi, gf, gft = _ab_in(xp, x_sample, mods[0], row(norm1_g[0]), w_main, wqvt,
                                          wg, wgft, bg, bgft)

    c0 = state_mlstm_C[:, 0].reshape(n_lat, N_STREAMS, M_DK, M_DK)
    n0 = state_mlstm_n[:, 0].reshape(n_lat, N_STREAMS, M_DK)
    m0 = jnp.pad(state_mlstm_m[:, 0].reshape(n_lat, N_STREAMS), ((0, 0), (0, pad)))
    conv_w_rep = jnp.broadcast_to(conv_w[0][:, None, :], (CONV_WIDTH, SUBLANES, CONV_CH))
    conv_b_rep = jnp.broadcast_to(conv_b[0][None, :], (SUBLANES, CONV_CH))
    hf, hb, new_c, new_n, new_m, u_conv = _mlstm(
        qt, k, vt, gi, gf, gft, c0, n0, m0.reshape(n_lat, 1, 128),
        u, conv_w_rep, conv_b_rep, row(conv_ln_g[0]), row(conv_ln_b[0]))

    w_o = w_out_ab[0].astype(BF16)
    x = _l0_tail(u_conv, hf, hb, o, xp, x_sample, mods[0], row(mlstm_norm_g[0]),
                 w_o[:CONV_CH], w_o[CONV_CH:], row(norm2_g[0]),
                 w_ffn_in[0].astype(BF16), w_ffn_out[0].astype(BF16))

    cos, sin = _rope_tables()
    kvw = N_KV * HEAD_DIM
    qk_cols = (N_Q + N_KV) * HEAD_DIM
    w_c = jnp.concatenate([_paired_layout(w_in_c[0][:, :qk_cols]), w_in_c[0][:, qk_cols:]],
                          axis=1).astype(BF16)
    qa, ka, va, k_norm, v_raw = _c_in(x, mods[1], row(norm1_g[1]), w_c,
                                      _paired_layout(row(q_norm_g[0])),
                                      _paired_layout(row(k_norm_g[0])), cos, sin)
    k_norm = _paired_layout(k_norm)
    ck = _paired_layout(cache_k[:, 0].reshape(n_lat, PAST_LEN, kvw)).astype(BF16)
    cv = cache_v[:, 0].reshape(n_lat, PAST_LEN, kvw).astype(BF16)
    oa = _attention(qa, ka, va, ck, cv)
    yp, y_sample = _l1_tail(oa, x, mods[1], w_out_c[0].astype(BF16), row(norm2_g[1]),
                            w_ffn_in[1].astype(BF16), w_ffn_out[1].astype(BF16),
                            row(final_norm_g))

    y_prompt = yp.reshape(n_ctx, CTX_SEQ, D_MODEL)
    new_state_c = new_c.reshape(n_ctx, 1, 2, M_HEADS, M_DK, M_DK)
    new_state_n = new_n.reshape(n_ctx, 1, 2, M_HEADS, M_DK)
    new_state_m = new_m[:, 0, :N_STREAMS].reshape(n_ctx, 1, 2, M_HEADS)
    new_k = k_norm.reshape(n_ctx, 1, CTX_SEQ, N_KV, HEAD_DIM)
    new_v = v_raw.reshape(n_ctx, 1, CTX_SEQ, N_KV, HEAD_DIM)
    return (y_prompt, y_sample, new_state_c, new_state_n, new_state_m, new_k, new_v)
```

```python
import jax
import jax.numpy as jnp
from jax import lax
from jax.experimental import pallas as pl
from jax.experimental.pallas import tpu as pltpu

F32 = jnp.float32
BF16 = jnp.bfloat16

D_MODEL = 1024
N_GROUPS = 5
GROUP_TOKENS = 4096
CTX_SEQ = 256
CONV_CH = 512
CONV_WIDTH = 31
M_HEADS = 4
M_DK = 128
M_WIDTH = 512
N_STREAMS = 2 * M_HEADS
STATE_ROWS = M_DK + 16
HEAD_DIM = 128
N_Q = 8
N_KV = 2
GROUP = N_Q // N_KV
PAST_LEN = 256
GRID_W = 64
ROPE_THETA = 10000.0
D_FF = 2816
EPS = 1e-6
LOG2_E = 1.4426950408889634

CHUNK = 256
TOKEN_TILE = 512
ROPE_TILE = 256
ATTN_TILE = 512
ATTN_CHAIN_ROWS = 128
FFN_CUTS = (0, 1536, D_FF)
FFN_CHAIN_ROWS = 512
HALO = 16
SUBLANES = 8
VMEM_LIMIT = 56 * 1024 * 1024

_ARB2 = pltpu.CompilerParams(dimension_semantics=("arbitrary", "arbitrary"),
                             vmem_limit_bytes=VMEM_LIMIT)
_ARB3 = pltpu.CompilerParams(dimension_semantics=("arbitrary", "arbitrary", "arbitrary"),
                             vmem_limit_bytes=VMEM_LIMIT)


def _rms(x):
    return x * lax.rsqrt(jnp.mean(x * x, axis=-1, keepdims=True) + EPS)


def _modulated(x, gain, shift, scale):
    return (_rms(x) * gain) * (1.0 + scale) + shift


def _sigmoid(x):
    return 1.0 / (1.0 + jnp.exp(-x))


def _log_sigmoid(x):
    return jnp.minimum(x, 0.0) - jnp.log1p(jnp.exp(-jnp.abs(x)))


def _dot(a, b):
    return jnp.dot(a, b, preferred_element_type=F32)


def _dot_nt(a, b):
    return lax.dot_general(a, b, (((1,), (1,)), ((), ())), preferred_element_type=F32)


def _prefix_sum(x, axis):
    pos = lax.broadcasted_iota(jnp.int32, x.shape, axis)
    k = 1
    while k < x.shape[axis]:
        x = x + jnp.where(pos >= k, pltpu.roll(x, k, axis), 0.0)
        k *= 2
    return x


def _resident_spec(a):
    return pl.BlockSpec(a.shape, lambda *_: (0,) * a.ndim, pipeline_mode=pl.Buffered(1))


def _resident_layer_spec(a, layer):
    return pl.BlockSpec((None,) + a.shape[1:], lambda *_: (layer,) + (0,) * (a.ndim - 1),
                        pipeline_mode=pl.Buffered(1))


def _ctx_spec(tile, nt, width):
    return pl.BlockSpec((None, tile, width), lambda g, t: (0, jnp.where(g == 0, t, nt - 1), 0))


def _lat_spec(tile, width):
    return pl.BlockSpec((None, tile, width),
                        lambda g, t: (jnp.maximum(g - 1, 0), jnp.where(g > 0, t, 0), 0))


def _mod_kernel(cond_ref, w_ref, b_ref, o_ref):
    c = cond_ref[...]
    s = (c * _sigmoid(c)).astype(BF16)
    o_ref[...] = _dot(s, w_ref[...].astype(BF16)) + b_ref[...]


def _adaln_all(cond8, w_mod, b_mod):
    depth, d, n = w_mod.shape
    tn = 1024
    return pl.pallas_call(
        _mod_kernel,
        grid=(depth, n // tn),
        in_specs=[
            pl.BlockSpec((8, d), lambda l, j: (0, 0)),
            pl.BlockSpec((None, d, tn), lambda l, j: (l, 0, j)),
            pl.BlockSpec((None, 1, tn), lambda l, j: (l, 0, j)),
        ],
        out_specs=pl.BlockSpec((None, 8, tn), lambda l, j: (l, 0, j)),
        out_shape=jax.ShapeDtypeStruct((depth, 8, n), F32),
        compiler_params=_ARB2,
        name="adaln_mod",
    )(cond8, w_mod, b_mod.reshape(depth, 1, n))


def _ab_in_kernel(xp_ref, xs_ref, mod_ref, g_ref, w_ref, wqvt_ref,
                  wg_ref, wgft_ref, bg_ref, bgft_ref,
                  u_ref, qt_ref, k_ref, vt_ref, o_ref, gi_ref, gf_ref, gft_ref):
    x = jnp.where(pl.program_id(0) == 0, xp_ref[...], xs_ref[...])
    h = _modulated(x, g_ref[...], mod_ref[0:1, :], mod_ref[1:2, :])
    hb = h.astype(BF16)

    def proj(c):
        return _dot(hb, w_ref[:, c * CONV_CH:(c + 1) * CONV_CH])

    u_ref[...] = proj(0) * _sigmoid(proj(1))
    k_ref[...] = proj(3) * (M_DK ** -0.5)
    o_ref[...] = proj(5)
    qvt = _dot_nt(wqvt_ref[...], hb).astype(BF16)
    qt_ref[...] = qvt[:M_WIDTH]
    vt_ref[...] = qvt[M_WIDTH:]
    lane = lax.broadcasted_iota(jnp.int32, (x.shape[0], 128), 1)
    gates = _dot(hb, wg_ref[...]) + bg_ref[...]
    gi_ref[...] = gates[:, :128]
    gf_ref[...] = jnp.where(lane < N_STREAMS, _log_sigmoid(gates[:, 128:]), 0.0)
    gft_ref[...] = _log_sigmoid(_dot_nt(wgft_ref[...], hb) + bgft_ref[...])


def _ab_in(xp, xs, mod, gain, w_main, wqvt, wg, wgft, bg, bgft):
    tm = TOKEN_TILE
    nt = GROUP_TOKENS // tm
    tok = lambda width: pl.BlockSpec((None, tm, width), lambda g, t: (g, t, 0))
    feat = lambda rows: pl.BlockSpec((None, rows, tm), lambda g, t: (g, 0, t))
    const = _resident_spec
    tok_sds = lambda width: jax.ShapeDtypeStruct((N_GROUPS, GROUP_TOKENS, width), F32)
    feat_sds = lambda rows, dt: jax.ShapeDtypeStruct((N_GROUPS, rows, GROUP_TOKENS), dt)
    consts = (gain, w_main, wqvt, wg, wgft, bg, bgft)
    return pl.pallas_call(
        _ab_in_kernel,
        grid=(N_GROUPS, nt),
        in_specs=[_ctx_spec(tm, nt, D_MODEL), _lat_spec(tm, D_MODEL),
                  pl.BlockSpec((None, 6, D_MODEL), lambda g, t: (g, 0, 0))]
                 + [const(a) for a in consts],
        out_specs=[tok(CONV_CH), feat(M_WIDTH), tok(M_WIDTH), feat(M_WIDTH), tok(M_WIDTH),
                   tok(128), tok(128), feat(N_STREAMS)],
        out_shape=[tok_sds(CONV_CH), feat_sds(M_WIDTH, BF16), tok_sds(M_WIDTH),
                   feat_sds(M_WIDTH, BF16), tok_sds(M_WIDTH),
                   tok_sds(128), tok_sds(128), feat_sds(N_STREAMS, F32)],
        compiler_params=_ARB2,
        name="ab_in_proj",
    )(xp, xs, mod, *consts)


def _mlstm_direction(qt_ref, k_ref, vt_ref, gi_ref, gf_ref, gft_ref, h_ref, cn_s, m_s, d,
                     side_jobs):
    L = k_ref.shape[0]
    r = lax.broadcasted_iota(jnp.int32, (L, L), 0)
    c = lax.broadcasted_iota(jnp.int32, (L, L), 1)
    le = r <= c
    ge = r >= c
    gf = gf_ref[...]
    gft = gft_ref[...]
    p_col = _prefix_sum(gf, 0)
    p_row = _prefix_sum(gft, 1)
    tot_col = p_col[L - 1:L, :]
    if d == 0:
        b_col, b_row, mask = p_col, p_row, le
    else:
        b_col = tot_col - p_col + gf
        b_row = p_row[:, L - 1:L] - p_row + gft
        mask = ge
    c_all = gi_ref[...] - b_col
    m_prev_all = m_s[...]
    m_fin = jnp.maximum(m_prev_all, jnp.max(c_all, axis=0, keepdims=True))
    w_all = jnp.exp(c_all - m_fin)
    a_all = jnp.exp(m_prev_all - m_fin)
    lane = lax.broadcasted_iota(jnp.int32, (1, 128), 1)
    mine = (lane >= d * M_HEADS) & (lane < (d + 1) * M_HEADS)
    m_s[...] = jnp.where(mine, tot_col + m_fin, m_prev_all)
    ones_row = jnp.where(lax.broadcasted_iota(jnp.int32, (STATE_ROWS - M_DK, L), 0) == 0,
                         1.0, 0.0).astype(BF16)
    for h in range(M_HEADS):
        s_idx = d * M_HEADS + h
        sl = slice(h * M_DK, (h + 1) * M_DK)
        c_col = c_all[:, s_idx:s_idx + 1]
        m_prev = m_prev_all[:, s_idx:s_idx + 1]
        cm = jnp.where(mask, c_col, -jnp.inf)
        m_row = jnp.maximum(jnp.max(cm, axis=0, keepdims=True), m_prev)
        dmat = jnp.exp(cm - m_row)
        k = k_ref[:, sl]
        cn = cn_s[s_idx]
        lhs = jnp.concatenate([k.astype(BF16), cn.astype(BF16)], axis=0)
        r1 = _dot(lhs, qt_ref[sl, :])
        st = (r1[0:L] * dmat).astype(BF16)
        vt_aug = jnp.concatenate([vt_ref[sl, :], ones_row], axis=0)
        w_inter = jnp.exp(m_prev - m_row)
        tot = _dot(vt_aug, st) + w_inter * r1[L:L + STATE_ROWS]
        den = tot[M_DK:M_DK + 1, :]
        m_t = b_row[s_idx:s_idx + 1, :] + m_row
        ht = tot[0:M_DK, :] / jnp.maximum(jnp.abs(den), jnp.exp(-m_t))
        h_ref[:, sl] = ht.T
        kw = (k * w_all[:, s_idx:s_idx + 1]).astype(BF16)
        cn_s[s_idx] = a_all[:, s_idx:s_idx + 1] * cn + _dot(vt_aug, kw)
        side_jobs.pop(0)()


def _mlstm_kernel(qtf_ref, kf_ref, vtf_ref, gif_ref, gff_ref, gftf_ref,
                  qtb_ref, kb_ref, vtb_ref, gib_ref, gfb_ref, gftb_ref,
                  c0_ref, n0_ref, m0_ref,
                  up_ref, uc_ref, un_ref, cw_ref, cb_ref, lg_ref, lb_ref,
                  hf_ref, hb_ref, co_ref, no_ref, mo_ref, uconv_ref,
                  cn_s, m_s, ext_ref):
    g = pl.program_id(0)
    c = pl.program_id(1)


    @pl.when(g == 0)
    def _():
        cn_s[...] = jnp.zeros_like(cn_s)
        m_s[...] = jnp.zeros_like(m_s)

    @pl.when((g > 0) & (c == 0))
    def _():
        cn_s[...] = jnp.zeros_like(cn_s)
        for s in range(N_STREAMS):
            cn_s[s, 0:M_DK, :] = c0_ref[s].T
            cn_s[s, M_DK:M_DK + 1, :] = n0_ref[s:s + 1, :]
        m_s[...] = m0_ref[...]

    jobs = _conv_jobs(g, c, pl.num_programs(1), up_ref, uc_ref, un_ref, cw_ref, cb_ref,
                      lg_ref, lb_ref, ext_ref, uconv_ref, N_STREAMS)
    jobs.pop(0)()
    _mlstm_direction(qtf_ref, kf_ref, vtf_ref, gif_ref, gff_ref, gftf_ref, hf_ref, cn_s, m_s, 0,
                     jobs)
    _mlstm_direction(qtb_ref, kb_ref, vtb_ref, gib_ref, gfb_ref, gftb_ref, hb_ref, cn_s, m_s, 1,
                     jobs)

    @pl.when(g == 0)
    def _():
        for s in range(N_STREAMS):
            co_ref[s] = cn_s[s, 0:M_DK, :].T
            no_ref[s:s + 1, :] = cn_s[s, M_DK:M_DK + 1, :]
        mo_ref[...] = m_s[...]


def _mlstm(qt, k, vt, gi, gf, gft, c0, n0, m0, u, conv_w, conv_b, ln_g, ln_b):
    L = CHUNK
    nc = GROUP_TOKENS // L
    n_ctx = GROUP_TOKENS // CTX_SEQ
    assert CTX_SEQ == L and n_ctx == nc
    per = L // HALO
    nh = GROUP_TOKENS // HALO
    conv_consts = (conv_w, conv_b, ln_g, ln_b)
    fwd = lambda g, c: c
    bwd = lambda g, c: jnp.where(g == 0, c, nc - 1 - c)
    tok = lambda width, pos: pl.BlockSpec((None, L, width), lambda g, c: (g, pos(g, c), 0))
    feat = lambda rows, pos: pl.BlockSpec((None, rows, L), lambda g, c: (g, 0, pos(g, c)))
    lat = lambda g, c: jnp.maximum(g - 1, 0)
    ctx = lambda g, c: jnp.where(g == 0, c, n_ctx - 1)
    per_dir = lambda pos: [feat(M_WIDTH, pos), tok(M_WIDTH, pos), feat(M_WIDTH, pos),
                           tok(128, pos), tok(128, pos), feat(N_STREAMS, pos)]
    return pl.pallas_call(
        _mlstm_kernel,
        grid=(N_GROUPS, nc),
        in_specs=per_dir(fwd) + per_dir(bwd) + [
            pl.BlockSpec((None, N_STREAMS, M_DK, M_DK), lambda g, c: (lat(g, c), 0, 0, 0)),
            pl.BlockSpec((None, N_STREAMS, M_DK), lambda g, c: (lat(g, c), 0, 0)),
            pl.BlockSpec((None, 1, 128), lambda g, c: (lat(g, c), 0, 0)),
            pl.BlockSpec((None, HALO, CONV_CH),
                         lambda g, c: (g, jnp.maximum(c * per - 1, 0), 0)),
            tok(CONV_CH, fwd),
            pl.BlockSpec((None, HALO, CONV_CH),
                         lambda g, c: (g, jnp.minimum((c + 1) * per, nh - 1), 0)),
        ] + [_resident_spec(a) for a in conv_consts],
        out_specs=[
            tok(M_WIDTH, fwd), tok(M_WIDTH, bwd),
            pl.BlockSpec((None, N_STREAMS, M_DK, M_DK), lambda g, c: (ctx(g, c), 0, 0, 0)),
            pl.BlockSpec((None, N_STREAMS, M_DK), lambda g, c: (ctx(g, c), 0, 0)),
            pl.BlockSpec((None, 1, 128), lambda g, c: (ctx(g, c), 0, 0)),
            tok(CONV_CH, fwd),
        ],
        out_shape=[
            jax.ShapeDtypeStruct((N_GROUPS, GROUP_TOKENS, M_WIDTH), F32),
            jax.ShapeDtypeStruct((N_GROUPS, GROUP_TOKENS, M_WIDTH), F32),
            jax.ShapeDtypeStruct((n_ctx, N_STREAMS, M_DK, M_DK), F32),
            jax.ShapeDtypeStruct((n_ctx, N_STREAMS, M_DK), F32),
            jax.ShapeDtypeStruct((n_ctx, 1, 128), F32),
            jax.ShapeDtypeStruct((N_GROUPS, GROUP_TOKENS, CONV_CH), BF16),
        ],
        scratch_shapes=[
            pltpu.VMEM((N_STREAMS, STATE_ROWS, M_DK), F32),
            pltpu.VMEM((1, 128), F32),
            pltpu.VMEM((SUBLANES, L + 2 * HALO, CONV_CH), F32),
        ],
        compiler_params=_ARB2,
        name="mlstm_scan",
    )(qt, k, vt, gi, gf, gft, qt, k, vt, gi, gf, gft, c0, n0, m0, u, u, u, *conv_consts)


def _conv_jobs(g, t, nt, up_ref, uc_ref, un_ref, cw_ref, cb_ref, lg_ref, lb_ref,
               ext_ref, out_ref, n_blocks):
    T = uc_ref.shape[0]
    rows = T // n_blocks
    span = T + 2 * HALO - SUBLANES
    off = HALO - CONV_WIDTH // 2

    def prepare():
        prev_on = jnp.where((g > 0) & (t > 0), 1.0, 0.0).astype(F32)
        next_on = jnp.where((g > 0) & (t < nt - 1), 1.0, 0.0).astype(F32)
        ext_ref[0, 0:HALO, :] = up_ref[...] * prev_on
        ext_ref[0, HALO:HALO + T, :] = uc_ref[...]
        ext_ref[0, HALO + T:HALO + T + HALO, :] = un_ref[...] * next_on
        for s in range(1, SUBLANES):
            ext_ref[s, 0:span, :] = ext_ref[0, s:s + span, :]

    def block(r0):
        def run():
            groups = rows // SUBLANES
            accs = [cb_ref[...]] * groups
            for j in range(CONV_WIDTH):
                phase, base = (off + j) % SUBLANES, (off + j) // SUBLANES * SUBLANES
                w = cw_ref[j]
                for q in range(groups):
                    lo = r0 + base + q * SUBLANES
                    accs[q] = accs[q] + w * ext_ref[phase, lo:lo + SUBLANES, :]
            y = jnp.concatenate(accs, axis=0)
            mu = jnp.mean(y, axis=-1, keepdims=True)
            yc = y - mu
            var = jnp.mean(yc * yc, axis=-1, keepdims=True)
            y = yc * lax.rsqrt(var + EPS) * lg_ref[...] + lb_ref[...]
            out_ref[r0:r0 + rows, :] = (y * _sigmoid(y)).astype(BF16)
        return run

    return [prepare] + [block(r0) for r0 in range(0, T, rows)]


def _mix_stage(hf_ref, hb_ref, o_ref, mg_ref, rows):
    hm = hf_ref[rows, :] + hb_ref[rows, :]
    hm = jnp.concatenate(
        [_rms(hm[:, h * M_DK:(h + 1) * M_DK]) for h in range(M_HEADS)], axis=-1)
    return (_sigmoid(o_ref[rows, :]) * (hm * mg_ref[...])).astype(BF16)


def _ffn_tail(x, mod_ref, g_ref, win_ref, wout_ref):
    hb = _modulated(x, g_ref[...], mod_ref[3:4, :], mod_ref[4:5, :]).astype(BF16)
    acc = jnp.zeros(x.shape, F32)
    for lo, hi in zip(FFN_CUTS[:-1], FFN_CUTS[1:]):
        gt = _dot(hb, win_ref[:, lo:hi])
        up = _dot(hb, win_ref[:, D_FF + lo:D_FF + hi])
        act = (gt * _sigmoid(gt) * up).astype(BF16)
        acc = acc + _dot(act, wout_ref[lo:hi, :])
    return x + mod_ref[5:6, :] * acc


def _l0_tail_kernel(u_ref, hf_ref, hb_ref, o_ref, xp_ref, xs_ref, mod_ref,
                    mg_ref, wu_ref, wm_ref, g2_ref, win_ref, wout_ref, out_ref):
    is_ctx = pl.program_id(0) == 0
    for r0 in range(0, u_ref.shape[0], FFN_CHAIN_ROWS):
        rows = slice(r0, r0 + FFN_CHAIN_ROWS)
        hm = _mix_stage(hf_ref, hb_ref, o_ref, mg_ref, rows)
        x = jnp.where(is_ctx, xp_ref[rows, :], xs_ref[rows, :])
        out = _dot(u_ref[rows, :], wu_ref[...]) + _dot(hm, wm_ref[...])
        out_ref[rows, :] = _ffn_tail(x + mod_ref[2:3, :] * out, mod_ref, g2_ref,
                                     win_ref, wout_ref)


def _l0_tail(u_conv, hf, hb, o, xp, xs, mod, mnorm_g, w_u, w_m, gain2, w_in_all, w_out_all,
             layer):
    T = TOKEN_TILE
    nt = GROUP_TOKENS // T
    tok = lambda width: pl.BlockSpec((None, T, width), lambda g, t: (g, t, 0))
    consts = (mnorm_g, w_u, w_m, gain2)
    stacked = (w_in_all, w_out_all)
    return pl.pallas_call(
        _l0_tail_kernel,
        grid=(N_GROUPS, nt),
        in_specs=[
            tok(CONV_CH), tok(M_WIDTH), tok(M_WIDTH), tok(M_WIDTH),
            _ctx_spec(T, nt, D_MODEL), _lat_spec(T, D_MODEL),
            pl.BlockSpec((None, 6, D_MODEL), lambda g, t: (g, 0, 0)),
        ] + [_resident_spec(a) for a in consts]
          + [_resident_layer_spec(a, layer) for a in stacked],
        out_specs=tok(D_MODEL),
        out_shape=jax.ShapeDtypeStruct((N_GROUPS, GROUP_TOKENS, D_MODEL), F32),
        compiler_params=_ARB2,
        name="l0_mix_ffn",
    )(u_conv, hf, hb, o, xp, xs, mod, *consts, *stacked)


def _l1_tail_kernel(o_ref, x_ref, mod_ref, wc_ref, g2_ref, win_ref, wout_ref, fg_ref,
                    yp_ref, ys_ref):
    x = x_ref[...] + mod_ref[2:3, :] * _dot(o_ref[...], wc_ref[...])
    y = _rms(_ffn_tail(x, mod_ref, g2_ref, win_ref, wout_ref)) * fg_ref[...]
    g = pl.program_id(0)

    @pl.when(g == 0)
    def _():
        yp_ref[...] = y

    @pl.when(g > 0)
    def _():
        ys_ref[...] = y


def _l1_tail(o, x, mod, w_c, gain2, w_in_all, w_out_all, layer, final_gain):
    tm = TOKEN_TILE
    nt = GROUP_TOKENS // tm
    tok = pl.BlockSpec((None, tm, D_MODEL), lambda g, t: (g, t, 0))
    consts = (w_c, gain2, w_in_all, w_out_all, final_gain)
    const_specs = [_resident_spec(w_c), _resident_spec(gain2),
                   _resident_layer_spec(w_in_all, layer), _resident_layer_spec(w_out_all, layer),
                   _resident_spec(final_gain)]
    return pl.pallas_call(
        _l1_tail_kernel,
        grid=(N_GROUPS, nt),
        in_specs=[tok, tok, pl.BlockSpec((None, 6, D_MODEL), lambda g, t: (g, 0, 0))]
                 + const_specs,
        out_specs=[_ctx_spec(tm, nt, D_MODEL), _lat_spec(tm, D_MODEL)],
        out_shape=[jax.ShapeDtypeStruct((1, GROUP_TOKENS, D_MODEL), F32),
                   jax.ShapeDtypeStruct((N_GROUPS - 1, GROUP_TOKENS, D_MODEL), F32)],
        compiler_params=_ARB2,
        name="l1_proj_ffn",
    )(o, x, mod, *consts)


def _paired_layout(a):
    quarter = HEAD_DIM // 4
    b = a.reshape(a.shape[:-1] + (a.shape[-1] // HEAD_DIM, 4, quarter))
    b = jnp.stack([b[..., 0, :], b[..., 2, :], b[..., 1, :], b[..., 3, :]], axis=-2)
    return b.reshape(a.shape)


def _rope(x, cos, sin):
    return x * cos + pltpu.roll(x, HEAD_DIM // 2, 1) * sin


def _c_in_kernel(x_ref, mod_ref, g_ref, w_ref, qg_ref, kg_ref, cos_ref, sin_ref,
                 q_ref, k_ref, v_ref, kn_ref, vn_ref):
    hb = _modulated(x_ref[...], g_ref[...], mod_ref[0:1, :], mod_ref[1:2, :]).astype(BF16)
    latent = pl.program_id(0) > 0
    cos = jnp.where(latent, cos_ref[...], 1.0)
    sin = jnp.where(latent, sin_ref[...], 0.0)
    q = _dot(hb, w_ref[:, 0:N_Q * HEAD_DIM])
    qg = qg_ref[...] * (HEAD_DIM ** -0.5 * LOG2_E)
    for h in range(N_Q):
        lanes = slice(h * HEAD_DIM, (h + 1) * HEAD_DIM)
        q_ref[:, lanes] = _rope(_rms(q[:, lanes]) * qg, cos, sin).astype(BF16)
    kv = _dot(hb, w_ref[:, N_Q * HEAD_DIM:(N_Q + 2 * N_KV) * HEAD_DIM])
    kg = kg_ref[...]
    kn = [_rms(kv[:, h * HEAD_DIM:(h + 1) * HEAD_DIM]) * kg for h in range(N_KV)]
    for h in range(N_KV):
        k_ref[:, h * HEAD_DIM:(h + 1) * HEAD_DIM] = _rope(kn[h], cos, sin).astype(BF16)
    v = kv[:, N_KV * HEAD_DIM:]
    v_ref[...] = v.astype(BF16)

    @pl.when(pl.program_id(0) == 0)
    def _():
        for h in range(N_KV):
            kn_ref[:, h * HEAD_DIM:(h + 1) * HEAD_DIM] = kn[h]
        vn_ref[...] = v


def _c_in(x, mod, gain, w, qg, kg, cos, sin):
    tm = ROPE_TILE
    kvw = N_KV * HEAD_DIM
    tok = lambda width: pl.BlockSpec((None, tm, width), lambda g, t: (g, t, 0))
    const = _resident_spec
    table = pl.BlockSpec((tm, HEAD_DIM), lambda g, t: (t, 0))
    nt = GROUP_TOKENS // tm
    ctx_only = pl.BlockSpec((tm, kvw), lambda g, t: (jnp.where(g == 0, t, nt - 1), 0))
    sds = lambda width, dt: jax.ShapeDtypeStruct((N_GROUPS, GROUP_TOKENS, width), dt)
    return pl.pallas_call(
        _c_in_kernel,
        grid=(N_GROUPS, GROUP_TOKENS // tm),
        in_specs=[tok(D_MODEL), pl.BlockSpec((None, 6, D_MODEL), lambda g, t: (g, 0, 0)),
                  const(gain), const(w), const(qg), const(kg), table, table],
        out_specs=[tok(N_Q * HEAD_DIM), tok(kvw), tok(kvw), ctx_only, ctx_only],
        out_shape=[sds(N_Q * HEAD_DIM, BF16), sds(kvw, BF16), sds(kvw, BF16),
                   jax.ShapeDtypeStruct((GROUP_TOKENS, kvw), F32),
                   jax.ShapeDtypeStruct((GROUP_TOKENS, kvw), F32)],
        compiler_params=_ARB2,
        name="attn_in_proj",
    )(x, mod, gain, w, qg, kg, cos, sin)


def _group_softmax_pv(q_ref, o_ref, rows, k, v_aug):
    for i in range(GROUP):
        lanes = slice(i * HEAD_DIM, (i + 1) * HEAD_DIM)
        s = _dot_nt(q_ref[rows, lanes], k)
        p = jnp.exp2(s - jnp.max(s, axis=-1, keepdims=True)).astype(BF16)
        o = _dot(p, v_aug)
        o_ref[rows, lanes] = (o[:, :HEAD_DIM] / o[:, HEAD_DIM:HEAD_DIM + 1]).astype(BF16)


def _ones_column(rows):
    lane = lax.broadcasted_iota(jnp.int32, (rows, HEAD_DIM), 1)
    return jnp.where(lane == 0, 1.0, 0.0).astype(BF16)


def _attn_kernel(q_ref, k_ref, v_ref, kc_ref, vc_ref, o_ref, kall_ref, vall_ref):
    g = pl.program_id(0)
    t = pl.program_id(2)
    T = q_ref.shape[0]
    S = k_ref.shape[0]

    @pl.when(g == 0)
    def _():
        for j in range(T // CTX_SEQ):
            keys = pl.ds(pl.multiple_of(t * T + j * CTX_SEQ, CTX_SEQ), CTX_SEQ)
            v_aug = jnp.concatenate([v_ref[keys, :], _ones_column(CTX_SEQ)], axis=-1)
            _group_softmax_pv(q_ref, o_ref, slice(j * CTX_SEQ, (j + 1) * CTX_SEQ),
                              k_ref[keys, :], v_aug)

    @pl.when((g > 0) & (t == 0))
    def _():
        kall_ref[0:S, :] = k_ref[...]
        kall_ref[S:, :] = kc_ref[...]
        vall_ref[0:S, 0:HEAD_DIM] = v_ref[...]
        vall_ref[S:, 0:HEAD_DIM] = vc_ref[...]
        vall_ref[:, HEAD_DIM:] = _ones_column(vall_ref.shape[0])

    @pl.when(g > 0)
    def _():
        for r0 in range(0, T, ATTN_CHAIN_ROWS):
            _group_softmax_pv(q_ref, o_ref, slice(r0, r0 + ATTN_CHAIN_ROWS),
                              kall_ref[...], vall_ref[...])


def _attention(q, k, v, cache_k, cache_v):
    T = ATTN_TILE
    assert T % CTX_SEQ == 0
    gw = GROUP * HEAD_DIM
    qspec = pl.BlockSpec((None, T, gw), lambda g, h, t: (g, t, h))
    kspec = pl.BlockSpec((None, GROUP_TOKENS, HEAD_DIM), lambda g, h, t: (g, 0, h))
    cspec = pl.BlockSpec((None, PAST_LEN, HEAD_DIM),
                         lambda g, h, t: (jnp.maximum(g - 1, 0), 0, h))
    n_keys = GROUP_TOKENS + PAST_LEN
    return pl.pallas_call(
        _attn_kernel,
        grid=(N_GROUPS, N_KV, GROUP_TOKENS // T),
        in_specs=[qspec, kspec, kspec, cspec, cspec],
        out_specs=qspec,
        out_shape=jax.ShapeDtypeStruct(q.shape, BF16),
        scratch_shapes=[pltpu.VMEM((n_keys, HEAD_DIM), BF16),
                        pltpu.VMEM((n_keys, 2 * HEAD_DIM), BF16)],
        compiler_params=_ARB3,
        name="gqa_attention",
    )(q, k, v, cache_k, cache_v)


def _rope_tables():
    rows = GROUP_TOKENS // GRID_W
    assert rows == GRID_W
    n_freq = HEAD_DIM // 4
    inv = 1.0 / (ROPE_THETA ** (jnp.arange(n_freq, dtype=F32) / n_freq))
    ang = jnp.arange(GRID_W).astype(F32)[:, None] * inv
    by_row = lambda a: jnp.repeat(a, GRID_W, axis=0)
    by_col = lambda a: jnp.tile(a, (rows, 1))
    cos_r, cos_c = by_row(jnp.cos(ang)), by_col(jnp.cos(ang))
    sin_r, sin_c = by_row(jnp.sin(ang)), by_col(jnp.sin(ang))
    cos = jnp.concatenate([cos_r, cos_c, cos_r, cos_c], axis=-1)
    sin = jnp.concatenate([-sin_r, -sin_c, sin_r, sin_c], axis=-1)
    return cos, sin


def kernel(x_prompt, x_sample, c, state_mlstm_C, state_mlstm_n, state_mlstm_m, cache_k, cache_v,
           c_ctx, w_mod, b_mod, norm1_g, norm2_g, w_in_ab, b_gate_ab, conv_w, conv_b, conv_ln_g,
           conv_ln_b, mlstm_norm_g, w_out_ab, w_in_c, q_norm_g, k_norm_g, w_out_c, w_ffn_in,
           w_ffn_out, final_norm_g):
    n_lat = x_sample.shape[0]
    n_ctx = x_prompt.shape[0]
    xp = x_prompt.reshape(1, GROUP_TOKENS, D_MODEL)

    cond8 = jnp.concatenate([c_ctx[None, :], c, jnp.zeros((8 - 1 - n_lat, D_MODEL), F32)], axis=0)
    mods = _adaln_all(cond8, w_mod, b_mod)[:, :N_GROUPS].reshape(2, N_GROUPS, 6, D_MODEL)
    row = lambda a: a.reshape(1, -1)
    w_ffn_in_b = w_ffn_in.astype(BF16)
    w_ffn_out_b = w_ffn_out.astype(BF16)

    main_cols = 2 * CONV_CH + 4 * M_WIDTH
    w_ab = w_in_ab[0]
    w_main = w_ab[:, :main_cols].astype(BF16)
    q_cols = slice(2 * CONV_CH, 2 * CONV_CH + M_WIDTH)
    v_cols = slice(2 * CONV_CH + 2 * M_WIDTH, 2 * CONV_CH + 3 * M_WIDTH)
    wqvt = jnp.concatenate([w_ab[:, q_cols], w_ab[:, v_cols]], axis=1).T.astype(BF16)
    gate_w = w_ab[:, main_cols:].reshape(D_MODEL, 2, 2, M_HEADS)
    gate_b = b_gate_ab[0].reshape(2, 2, M_HEADS)
    pad = 128 - N_STREAMS
    lanes = lambda a: jnp.pad(a.reshape(-1, N_STREAMS), ((0, 0), (0, pad)))
    wg = jnp.concatenate([lanes(gate_w[:, :, 0]), lanes(gate_w[:, :, 1])], axis=1).astype(BF16)
    bg = jnp.concatenate([lanes(gate_b[:, 0]), lanes(gate_b[:, 1])], axis=1)
    wgft = gate_w[:, :, 1].reshape(D_MODEL, N_STREAMS).T.astype(BF16)
    bgft = gate_b[:, 1].reshape(N_STREAMS, 1)
    u, qt, k, vt, o, gi, gf, gft = _ab_in(xp, x_sample, mods[0], row(norm1_g[0]), w_main, wqvt,
                                          wg, wgft, bg, bgft)

    c0 = state_mlstm_C[:, 0].reshape(n_lat, N_STREAMS, M_DK, M_DK)
    n0 = state_mlstm_n[:, 0].reshape(n_lat, N_STREAMS, M_DK)
    m0 = jnp.pad(state_mlstm_m[:, 0].reshape(n_lat, N_STREAMS), ((0, 0), (0, pad)))
    conv_w_rep = jnp.broadcast_to(conv_w[0][:, None, :], (CONV_WIDTH, SUBLANES, CONV_CH))
    conv_b_rep = jnp.broadcast_to(conv_b[0][None, :], (SUBLANES, CONV_CH))
    hf, hb, new_c, new_n, new_m, u_conv = _mlstm(
        qt, k, vt, gi, gf, gft, c0, n0, m0.reshape(n_lat, 1, 128),
        u, conv_w_rep, conv_b_rep, row(conv_ln_g[0]), row(conv_ln_b[0]))

    w_o = w_out_ab[0].astype(BF16)
    x = _l0_tail(u_conv, hf, hb, o, xp, x_sample, mods[0], row(mlstm_norm_g[0]),
                 w_o[:CONV_CH], w_o[CONV_CH:], row(norm2_g[0]),
                 w_ffn_in_b, w_ffn_out_b, 0)

    cos, sin = _rope_tables()
    kvw = N_KV * HEAD_DIM
    qk_cols = (N_Q + N_KV) * HEAD_DIM
    w_c = jnp.concatenate([_paired_layout(w_in_c[0][:, :qk_cols]), w_in_c[0][:, qk_cols:]],
                          axis=1).astype(BF16)
    qa, ka, va, k_norm, v_raw = _c_in(x, mods[1], row(norm1_g[1]), w_c,
                                      _paired_layout(row(q_norm_g[0])),
                                      _paired_layout(row(k_norm_g[0])), cos, sin)
    k_norm = _paired_layout(k_norm)
    ck = _paired_layout(cache_k[:, 0].reshape(n_lat, PAST_LEN, kvw)).astype(BF16)
    cv = cache_v[:, 0].reshape(n_lat, PAST_LEN, kvw).astype(BF16)
    oa = _attention(qa, ka, va, ck, cv)
    yp, y_sample = _l1_tail(oa, x, mods[1], w_out_c[0].astype(BF16), row(norm2_g[1]),
                            w_ffn_in_b, w_ffn_out_b, 1, row(final_norm_g))

    y_prompt = yp.reshape(n_ctx, CTX_SEQ, D_MODEL)
    new_state_c = new_c.reshape(n_ctx, 1, 2, M_HEADS, M_DK, M_DK)
    new_state_n = new_n.reshape(n_ctx, 1, 2, M_HEADS, M_DK)
    new_state_m = new_m[:, 0, :N_STREAMS].reshape(n_ctx, 1, 2, M_HEADS)
    new_k = k_norm.reshape(n_ctx, 1, CTX_SEQ, N_KV, HEAD_DIM)
    new_v = v_raw.reshape(n_ctx, 1, CTX_SEQ, N_KV, HEAD_DIM)
    return (y_prompt, y_sample, new_state_c, new_state_n, new_state_m, new_k, new_v)
```

```python
import jax
import jax.numpy as jnp
from jax import lax
from jax.experimental import pallas as pl
from jax.experimental.pallas import tpu as pltpu

F32 = jnp.float32
BF16 = jnp.bfloat16

D_MODEL = 1024
N_GROUPS = 5
GROUP_TOKENS = 4096
CTX_SEQ = 256
CONV_CH = 512
CONV_WIDTH = 31
M_HEADS = 4
M_DK = 128
M_WIDTH = 512
N_STREAMS = 2 * M_HEADS
STATE_ROWS = M_DK + 16
HEAD_DIM = 128
N_Q = 8
N_KV = 2
GROUP = N_Q // N_KV
PAST_LEN = 256
GRID_W = 64
ROPE_THETA = 10000.0
D_FF = 2816
EPS = 1e-6
LOG2_E = 1.4426950408889634

CHUNK = 256
TOKEN_TILE = 512
ROPE_TILE = 256
ATTN_TILE = 1024
ATTN_CHAIN_ROWS = 128
FFN_CUTS = (0, 1536, D_FF)
HALO = 16
SUBLANES = 8
VMEM_LIMIT = 56 * 1024 * 1024

_ARB2 = pltpu.CompilerParams(dimension_semantics=("arbitrary", "arbitrary"),
                             vmem_limit_bytes=VMEM_LIMIT)
_ARB3 = pltpu.CompilerParams(dimension_semantics=("arbitrary", "arbitrary", "arbitrary"),
                             vmem_limit_bytes=VMEM_LIMIT)


def _rms(x):
    return x * lax.rsqrt(jnp.mean(x * x, axis=-1, keepdims=True) + EPS)


def _modulated(x, gain, shift, scale):
    return (_rms(x) * gain) * (1.0 + scale) + shift


def _sigmoid(x):
    return 1.0 / (1.0 + jnp.exp(-x))


def _log_sigmoid(x):
    return jnp.minimum(x, 0.0) - jnp.log1p(jnp.exp(-jnp.abs(x)))


def _dot(a, b):
    return jnp.dot(a, b, preferred_element_type=F32)


def _dot_nt(a, b):
    return lax.dot_general(a, b, (((1,), (1,)), ((), ())), preferred_element_type=F32)


def _prefix_sum(x, axis):
    pos = lax.broadcasted_iota(jnp.int32, x.shape, axis)
    k = 1
    while k < x.shape[axis]:
        x = x + jnp.where(pos >= k, pltpu.roll(x, k, axis), 0.0)
        k *= 2
    return x


def _resident_spec(a):
    return pl.BlockSpec(a.shape, lambda *_: (0,) * a.ndim, pipeline_mode=pl.Buffered(1))


def _resident_layer_spec(a, layer):
    return pl.BlockSpec((None,) + a.shape[1:], lambda *_: (layer,) + (0,) * (a.ndim - 1),
                        pipeline_mode=pl.Buffered(1))


def _ctx_spec(tile, nt, width):
    return pl.BlockSpec((None, tile, width), lambda g, t: (0, jnp.where(g == 0, t, nt - 1), 0))


def _lat_spec(tile, width):
    return pl.BlockSpec((None, tile, width),
                        lambda g, t: (jnp.maximum(g - 1, 0), jnp.where(g > 0, t, 0), 0))


def _mod_kernel(cond_ref, w_ref, b_ref, o_ref):
    c = cond_ref[...]
    s = (c * _sigmoid(c)).astype(BF16)
    o_ref[...] = _dot(s, w_ref[...].astype(BF16)) + b_ref[...]


def _adaln_all(cond8, w_mod, b_mod):
    depth, d, n = w_mod.shape
    tn = 1024
    return pl.pallas_call(
        _mod_kernel,
        grid=(depth, n // tn),
        in_specs=[
            pl.BlockSpec((8, d), lambda l, j: (0, 0)),
            pl.BlockSpec((None, d, tn), lambda l, j: (l, 0, j)),
            pl.BlockSpec((None, 1, tn), lambda l, j: (l, 0, j)),
        ],
        out_specs=pl.BlockSpec((None, 8, tn), lambda l, j: (l, 0, j)),
        out_shape=jax.ShapeDtypeStruct((depth, 8, n), F32),
        compiler_params=_ARB2,
        name="adaln_mod",
    )(cond8, w_mod, b_mod.reshape(depth, 1, n))


def _ab_in_kernel(xp_ref, xs_ref, mod_ref, g_ref, w_ref, wqvt_ref,
                  wg_ref, wgft_ref, bg_ref, bgft_ref,
                  u_ref, qt_ref, k_ref, vt_ref, o_ref, gi_ref, gf_ref, gft_ref):
    x = jnp.where(pl.program_id(0) == 0, xp_ref[...], xs_ref[...])
    h = _modulated(x, g_ref[...], mod_ref[0:1, :], mod_ref[1:2, :])
    hb = h.astype(BF16)

    def proj(c):
        return _dot(hb, w_ref[:, c * CONV_CH:(c + 1) * CONV_CH])

    u_ref[...] = proj(0) * _sigmoid(proj(1))
    k_ref[...] = proj(3) * (M_DK ** -0.5)
    o_ref[...] = proj(5)
    qvt = _dot_nt(wqvt_ref[...], hb).astype(BF16)
    qt_ref[...] = qvt[:M_WIDTH]
    vt_ref[...] = qvt[M_WIDTH:]
    lane = lax.broadcasted_iota(jnp.int32, (x.shape[0], 128), 1)
    gates = _dot(hb, wg_ref[...]) + bg_ref[...]
    gi_ref[...] = gates[:, :128]
    gf_ref[...] = jnp.where(lane < N_STREAMS, _log_sigmoid(gates[:, 128:]), 0.0)
    gft_ref[...] = _log_sigmoid(_dot_nt(wgft_ref[...], hb) + bgft_ref[...])


def _ab_in(xp, xs, mod, gain, w_main, wqvt, wg, wgft, bg, bgft):
    tm = TOKEN_TILE
    nt = GROUP_TOKENS // tm
    tok = lambda width: pl.BlockSpec((None, tm, width), lambda g, t: (g, t, 0))
    feat = lambda rows: pl.BlockSpec((None, rows, tm), lambda g, t: (g, 0, t))
    const = _resident_spec
    tok_sds = lambda width: jax.ShapeDtypeStruct((N_GROUPS, GROUP_TOKENS, width), F32)
    feat_sds = lambda rows, dt: jax.ShapeDtypeStruct((N_GROUPS, rows, GROUP_TOKENS), dt)
    consts = (gain, w_main, wqvt, wg, wgft, bg, bgft)
    return pl.pallas_call(
        _ab_in_kernel,
        grid=(N_GROUPS, nt),
        in_specs=[_ctx_spec(tm, nt, D_MODEL), _lat_spec(tm, D_MODEL),
                  pl.BlockSpec((None, 6, D_MODEL), lambda g, t: (g, 0, 0))]
                 + [const(a) for a in consts],
        out_specs=[tok(CONV_CH), feat(M_WIDTH), tok(M_WIDTH), feat(M_WIDTH), tok(M_WIDTH),
                   tok(128), tok(128), feat(N_STREAMS)],
        out_shape=[tok_sds(CONV_CH), feat_sds(M_WIDTH, BF16), tok_sds(M_WIDTH),
                   feat_sds(M_WIDTH, BF16), tok_sds(M_WIDTH),
                   tok_sds(128), tok_sds(128), feat_sds(N_STREAMS, F32)],
        compiler_params=_ARB2,
        name="ab_in_proj",
    )(xp, xs, mod, *consts)


def _mlstm_direction(qt_ref, k_ref, vt_ref, gi_ref, gf_ref, gft_ref, h_ref, cn_s, m_s, d,
                     side_jobs):
    L = k_ref.shape[0]
    r = lax.broadcasted_iota(jnp.int32, (L, L), 0)
    c = lax.broadcasted_iota(jnp.int32, (L, L), 1)
    le = r <= c
    ge = r >= c
    gf = gf_ref[...]
    gft = gft_ref[...]
    p_col = _prefix_sum(gf, 0)
    p_row = _prefix_sum(gft, 1)
    tot_col = p_col[L - 1:L, :]
    if d == 0:
        b_col, b_row, mask = p_col, p_row, le
    else:
        b_col = tot_col - p_col + gf
        b_row = p_row[:, L - 1:L] - p_row + gft
        mask = ge
    c_all = gi_ref[...] - b_col
    m_prev_all = m_s[...]
    m_fin = jnp.maximum(m_prev_all, jnp.max(c_all, axis=0, keepdims=True))
    w_all = jnp.exp(c_all - m_fin)
    a_all = jnp.exp(m_prev_all - m_fin)
    lane = lax.broadcasted_iota(jnp.int32, (1, 128), 1)
    mine = (lane >= d * M_HEADS) & (lane < (d + 1) * M_HEADS)
    m_s[...] = jnp.where(mine, tot_col + m_fin, m_prev_all)
    ones_row = jnp.where(lax.broadcasted_iota(jnp.int32, (STATE_ROWS - M_DK, L), 0) == 0,
                         1.0, 0.0).astype(BF16)
    for h in range(M_HEADS):
        s_idx = d * M_HEADS + h
        sl = slice(h * M_DK, (h + 1) * M_DK)
        c_col = c_all[:, s_idx:s_idx + 1]
        m_prev = m_prev_all[:, s_idx:s_idx + 1]
        cm = jnp.where(mask, c_col, -jnp.inf)
        m_row = jnp.maximum(jnp.max(cm, axis=0, keepdims=True), m_prev)
        dmat = jnp.exp(cm - m_row)
        k = k_ref[:, sl]
        cn = cn_s[s_idx]
        lhs = jnp.concatenate([k.astype(BF16), cn.astype(BF16)], axis=0)
        r1 = _dot(lhs, qt_ref[sl, :])
        st = (r1[0:L] * dmat).astype(BF16)
        vt_aug = jnp.concatenate([vt_ref[sl, :], ones_row], axis=0)
        w_inter = jnp.exp(m_prev - m_row)
        tot = _dot(vt_aug, st) + w_inter * r1[L:L + STATE_ROWS]
        den = tot[M_DK:M_DK + 1, :]
        m_t = b_row[s_idx:s_idx + 1, :] + m_row
        ht = tot[0:M_DK, :] / jnp.maximum(jnp.abs(den), jnp.exp(-m_t))
        h_ref[:, sl] = ht.T
        kw = (k * w_all[:, s_idx:s_idx + 1]).astype(BF16)
        cn_s[s_idx] = a_all[:, s_idx:s_idx + 1] * cn + _dot(vt_aug, kw)
        side_jobs.pop(0)()


def _mlstm_kernel(qtf_ref, kf_ref, vtf_ref, gif_ref, gff_ref, gftf_ref,
                  qtb_ref, kb_ref, vtb_ref, gib_ref, gfb_ref, gftb_ref,
                  c0_ref, n0_ref, m0_ref,
                  up_ref, uc_ref, un_ref, cw_ref, cb_ref, lg_ref, lb_ref,
                  hf_ref, hb_ref, co_ref, no_ref, mo_ref, uconv_ref,
                  cn_s, m_s, ext_ref):
    g = pl.program_id(0)
    c = pl.program_id(1)


    @pl.when(g == 0)
    def _():
        cn_s[...] = jnp.zeros_like(cn_s)
        m_s[...] = jnp.zeros_like(m_s)

    @pl.when((g > 0) & (c == 0))
    def _():
        cn_s[...] = jnp.zeros_like(cn_s)
        for s in range(N_STREAMS):
            cn_s[s, 0:M_DK, :] = c0_ref[s].T
            cn_s[s, M_DK:M_DK + 1, :] = n0_ref[s:s + 1, :]
        m_s[...] = m0_ref[...]

    jobs = _conv_jobs(g, c, pl.num_programs(1), up_ref, uc_ref, un_ref, cw_ref, cb_ref,
                      lg_ref, lb_ref, ext_ref, uconv_ref, N_STREAMS)
    jobs.pop(0)()
    _mlstm_direction(qtf_ref, kf_ref, vtf_ref, gif_ref, gff_ref, gftf_ref, hf_ref, cn_s, m_s, 0,
                     jobs)
    _mlstm_direction(qtb_ref, kb_ref, vtb_ref, gib_ref, gfb_ref, gftb_ref, hb_ref, cn_s, m_s, 1,
                     jobs)

    @pl.when(g == 0)
    def _():
        for s in range(N_STREAMS):
            co_ref[s] = cn_s[s, 0:M_DK, :].T
            no_ref[s:s + 1, :] = cn_s[s, M_DK:M_DK + 1, :]
        mo_ref[...] = m_s[...]


def _mlstm(qt, k, vt, gi, gf, gft, c0, n0, m0, u, conv_w, conv_b, ln_g, ln_b):
    L = CHUNK
    nc = GROUP_TOKENS // L
    n_ctx = GROUP_TOKENS // CTX_SEQ
    assert CTX_SEQ == L and n_ctx == nc
    per = L // HALO
    nh = GROUP_TOKENS // HALO
    conv_consts = (conv_w, conv_b, ln_g, ln_b)
    fwd = lambda g, c: c
    bwd = lambda g, c: jnp.where(g == 0, c, nc - 1 - c)
    tok = lambda width, pos: pl.BlockSpec((None, L, width), lambda g, c: (g, pos(g, c), 0))
    feat = lambda rows, pos: pl.BlockSpec((None, rows, L), lambda g, c: (g, 0, pos(g, c)))
    lat = lambda g, c: jnp.maximum(g - 1, 0)
    ctx = lambda g, c: jnp.where(g == 0, c, n_ctx - 1)
    per_dir = lambda pos: [feat(M_WIDTH, pos), tok(M_WIDTH, pos), feat(M_WIDTH, pos),
                           tok(128, pos), tok(128, pos), feat(N_STREAMS, pos)]
    return pl.pallas_call(
        _mlstm_kernel,
        grid=(N_GROUPS, nc),
        in_specs=per_dir(fwd) + per_dir(bwd) + [
            pl.BlockSpec((None, N_STREAMS, M_DK, M_DK), lambda g, c: (lat(g, c), 0, 0, 0)),
            pl.BlockSpec((None, N_STREAMS, M_DK), lambda g, c: (lat(g, c), 0, 0)),
            pl.BlockSpec((None, 1, 128), lambda g, c: (lat(g, c), 0, 0)),
            pl.BlockSpec((None, HALO, CONV_CH),
                         lambda g, c: (g, jnp.maximum(c * per - 1, 0), 0)),
            tok(CONV_CH, fwd),
            pl.BlockSpec((None, HALO, CONV_CH),
                         lambda g, c: (g, jnp.minimum((c + 1) * per, nh - 1), 0)),
        ] + [_resident_spec(a) for a in conv_consts],
        out_specs=[
            tok(M_WIDTH, fwd), tok(M_WIDTH, bwd),
            pl.BlockSpec((None, N_STREAMS, M_DK, M_DK), lambda g, c: (ctx(g, c), 0, 0, 0)),
            pl.BlockSpec((None, N_STREAMS, M_DK), lambda g, c: (ctx(g, c), 0, 0)),
            pl.BlockSpec((None, 1, 128), lambda g, c: (ctx(g, c), 0, 0)),
            tok(CONV_CH, fwd),
        ],
        out_shape=[
            jax.ShapeDtypeStruct((N_GROUPS, GROUP_TOKENS, M_WIDTH), F32),
            jax.ShapeDtypeStruct((N_GROUPS, GROUP_TOKENS, M_WIDTH), F32),
            jax.ShapeDtypeStruct((n_ctx, N_STREAMS, M_DK, M_DK), F32),
            jax.ShapeDtypeStruct((n_ctx, N_STREAMS, M_DK), F32),
            jax.ShapeDtypeStruct((n_ctx, 1, 128), F32),
            jax.ShapeDtypeStruct((N_GROUPS, GROUP_TOKENS, CONV_CH), BF16),
        ],
        scratch_shapes=[
            pltpu.VMEM((N_STREAMS, STATE_ROWS, M_DK), F32),
            pltpu.VMEM((1, 128), F32),
            pltpu.VMEM((SUBLANES, L + 2 * HALO, CONV_CH), F32),
        ],
        compiler_params=_ARB2,
        name="mlstm_scan",
    )(qt, k, vt, gi, gf, gft, qt, k, vt, gi, gf, gft, c0, n0, m0, u, u, u, *conv_consts)


def _conv_jobs(g, t, nt, up_ref, uc_ref, un_ref, cw_ref, cb_ref, lg_ref, lb_ref,
               ext_ref, out_ref, n_blocks):
    T = uc_ref.shape[0]
    rows = T // n_blocks
    span = T + 2 * HALO - SUBLANES
    off = HALO - CONV_WIDTH // 2

    def prepare():
        prev_on = jnp.where((g > 0) & (t > 0), 1.0, 0.0).astype(F32)
        next_on = jnp.where((g > 0) & (t < nt - 1), 1.0, 0.0).astype(F32)
        ext_ref[0, 0:HALO, :] = up_ref[...] * prev_on
        ext_ref[0, HALO:HALO + T, :] = uc_ref[...]
        ext_ref[0, HALO + T:HALO + T + HALO, :] = un_ref[...] * next_on
        for s in range(1, SUBLANES):
            ext_ref[s, 0:span, :] = ext_ref[0, s:s + span, :]

    def block(r0):
        def run():
            groups = rows // SUBLANES
            accs = [cb_ref[...]] * groups
            for j in range(CONV_WIDTH):
                phase, base = (off + j) % SUBLANES, (off + j) // SUBLANES * SUBLANES
                w = cw_ref[j]
                for q in range(groups):
                    lo = r0 + base + q * SUBLANES
                    accs[q] = accs[q] + w * ext_ref[phase, lo:lo + SUBLANES, :]
            y = jnp.concatenate(accs, axis=0)
            mu = jnp.mean(y, axis=-1, keepdims=True)
            yc = y - mu
            var = jnp.mean(yc * yc, axis=-1, keepdims=True)
            y = yc * lax.rsqrt(var + EPS) * lg_ref[...] + lb_ref[...]
            out_ref[r0:r0 + rows, :] = (y * _sigmoid(y)).astype(BF16)
        return run

    return [prepare] + [block(r0) for r0 in range(0, T, rows)]


def _mix_stage(hf_ref, hb_ref, o_ref, mg_ref):
    hm = hf_ref[...] + hb_ref[...]
    hm = jnp.concatenate(
        [_rms(hm[:, h * M_DK:(h + 1) * M_DK]) for h in range(M_HEADS)], axis=-1)
    return (_sigmoid(o_ref[...]) * (hm * mg_ref[...])).astype(BF16)


def _ffn_tail(x, mod_ref, g_ref, win_ref, wout_ref):
    hb = _modulated(x, g_ref[...], mod_ref[3:4, :], mod_ref[4:5, :]).astype(BF16)
    acc = jnp.zeros(x.shape, F32)
    for lo, hi in zip(FFN_CUTS[:-1], FFN_CUTS[1:]):
        gt = _dot(hb, win_ref[:, lo:hi])
        up = _dot(hb, win_ref[:, D_FF + lo:D_FF + hi])
        act = (gt * _sigmoid(gt) * up).astype(BF16)
        acc = acc + _dot(act, wout_ref[lo:hi, :])
    return x + mod_ref[5:6, :] * acc


def _l0_tail_kernel(u_ref, hf_ref, hb_ref, o_ref, xp_ref, xs_ref, mod_ref,
                    mg_ref, wu_ref, wm_ref, g2_ref, win_ref, wout_ref, out_ref):
    hm = _mix_stage(hf_ref, hb_ref, o_ref, mg_ref)
    x = jnp.where(pl.program_id(0) == 0, xp_ref[...], xs_ref[...])
    out = _dot(u_ref[...], wu_ref[...]) + _dot(hm, wm_ref[...])
    out_ref[...] = _ffn_tail(x + mod_ref[2:3, :] * out, mod_ref, g2_ref, win_ref, wout_ref)


def _l0_tail(u_conv, hf, hb, o, xp, xs, mod, mnorm_g, w_u, w_m, gain2, w_in_all, w_out_all,
             layer):
    T = TOKEN_TILE
    nt = GROUP_TOKENS // T
    tok = lambda width: pl.BlockSpec((None, T, width), lambda g, t: (g, t, 0))
    consts = (mnorm_g, w_u, w_m, gain2)
    stacked = (w_in_all, w_out_all)
    return pl.pallas_call(
        _l0_tail_kernel,
        grid=(N_GROUPS, nt),
        in_specs=[
            tok(CONV_CH), tok(M_WIDTH), tok(M_WIDTH), tok(M_WIDTH),
            _ctx_spec(T, nt, D_MODEL), _lat_spec(T, D_MODEL),
            pl.BlockSpec((None, 6, D_MODEL), lambda g, t: (g, 0, 0)),
        ] + [_resident_spec(a) for a in consts]
          + [_resident_layer_spec(a, layer) for a in stacked],
        out_specs=tok(D_MODEL),
        out_shape=jax.ShapeDtypeStruct((N_GROUPS, GROUP_TOKENS, D_MODEL), F32),
        compiler_params=_ARB2,
        name="l0_mix_ffn",
    )(u_conv, hf, hb, o, xp, xs, mod, *consts, *stacked)


def _l1_tail_kernel(o_ref, x_ref, mod_ref, wc_ref, g2_ref, win_ref, wout_ref, fg_ref,
                    yp_ref, ys_ref):
    x = x_ref[...] + mod_ref[2:3, :] * _dot(o_ref[...], wc_ref[...])
    y = _rms(_ffn_tail(x, mod_ref, g2_ref, win_ref, wout_ref)) * fg_ref[...]
    g = pl.program_id(0)

    @pl.when(g == 0)
    def _():
        yp_ref[...] = y

    @pl.when(g > 0)
    def _():
        ys_ref[...] = y


def _l1_tail(o, x, mod, w_c, gain2, w_in_all, w_out_all, layer, final_gain):
    tm = TOKEN_TILE
    nt = GROUP_TOKENS // tm
    tok = pl.BlockSpec((None, tm, D_MODEL), lambda g, t: (g, t, 0))
    consts = (w_c, gain2, w_in_all, w_out_all, final_gain)
    const_specs = [_resident_spec(w_c), _resident_spec(gain2),
                   _resident_layer_spec(w_in_all, layer), _resident_layer_spec(w_out_all, layer),
                   _resident_spec(final_gain)]
    return pl.pallas_call(
        _l1_tail_kernel,
        grid=(N_GROUPS, nt),
        in_specs=[tok, tok, pl.BlockSpec((None, 6, D_MODEL), lambda g, t: (g, 0, 0))]
                 + const_specs,
        out_specs=[_ctx_spec(tm, nt, D_MODEL), _lat_spec(tm, D_MODEL)],
        out_shape=[jax.ShapeDtypeStruct((1, GROUP_TOKENS, D_MODEL), F32),
                   jax.ShapeDtypeStruct((N_GROUPS - 1, GROUP_TOKENS, D_MODEL), F32)],
        compiler_params=_ARB2,
        name="l1_proj_ffn",
    )(o, x, mod, *consts)


def _paired_layout(a):
    quarter = HEAD_DIM // 4
    b = a.reshape(a.shape[:-1] + (a.shape[-1] // HEAD_DIM, 4, quarter))
    b = jnp.stack([b[..., 0, :], b[..., 2, :], b[..., 1, :], b[..., 3, :]], axis=-2)
    return b.reshape(a.shape)


def _rope(x, cos, sin):
    return x * cos + pltpu.roll(x, HEAD_DIM // 2, 1) * sin


def _c_in_kernel(x_ref, mod_ref, g_ref, w_ref, qg_ref, kg_ref, cos_ref, sin_ref,
                 q_ref, k_ref, v_ref, kn_ref, vn_ref):
    hb = _modulated(x_ref[...], g_ref[...], mod_ref[0:1, :], mod_ref[1:2, :]).astype(BF16)
    latent = pl.program_id(0) > 0
    cos = jnp.where(latent, cos_ref[...], 1.0)
    sin = jnp.where(latent, sin_ref[...], 0.0)
    q = _dot(hb, w_ref[:, 0:N_Q * HEAD_DIM])
    qg = qg_ref[...] * (HEAD_DIM ** -0.5 * LOG2_E)
    for h in range(N_Q):
        lanes = slice(h * HEAD_DIM, (h + 1) * HEAD_DIM)
        q_ref[:, lanes] = _rope(_rms(q[:, lanes]) * qg, cos, sin).astype(BF16)
    kv = _dot(hb, w_ref[:, N_Q * HEAD_DIM:(N_Q + 2 * N_KV) * HEAD_DIM])
    kg = kg_ref[...]
    kn = [_rms(kv[:, h * HEAD_DIM:(h + 1) * HEAD_DIM]) * kg for h in range(N_KV)]
    for h in range(N_KV):
        k_ref[:, h * HEAD_DIM:(h + 1) * HEAD_DIM] = _rope(kn[h], cos, sin).astype(BF16)
    v = kv[:, N_KV * HEAD_DIM:]
    v_ref[...] = v.astype(BF16)

    @pl.when(pl.program_id(0) == 0)
    def _():
        for h in range(N_KV):
            kn_ref[:, h * HEAD_DIM:(h + 1) * HEAD_DIM] = kn[h]
        vn_ref[...] = v


def _c_in(x, mod, gain, w, qg, kg, cos, sin):
    tm = ROPE_TILE
    kvw = N_KV * HEAD_DIM
    tok = lambda width: pl.BlockSpec((None, tm, width), lambda g, t: (g, t, 0))
    const = _resident_spec
    table = pl.BlockSpec((tm, HEAD_DIM), lambda g, t: (t, 0))
    nt = GROUP_TOKENS // tm
    ctx_only = pl.BlockSpec((tm, kvw), lambda g, t: (jnp.where(g == 0, t, nt - 1), 0))
    sds = lambda width, dt: jax.ShapeDtypeStruct((N_GROUPS, GROUP_TOKENS, width), dt)
    return pl.pallas_call(
        _c_in_kernel,
        grid=(N_GROUPS, GROUP_TOKENS // tm),
        in_specs=[tok(D_MODEL), pl.BlockSpec((None, 6, D_MODEL), lambda g, t: (g, 0, 0)),
                  const(gain), const(w), const(qg), const(kg), table, table],
        out_specs=[tok(N_Q * HEAD_DIM), tok(kvw), tok(kvw), ctx_only, ctx_only],
        out_shape=[sds(N_Q * HEAD_DIM, BF16), sds(kvw, BF16), sds(kvw, BF16),
                   jax.ShapeDtypeStruct((GROUP_TOKENS, kvw), F32),
                   jax.ShapeDtypeStruct((GROUP_TOKENS, kvw), F32)],
        compiler_params=_ARB2,
        name="attn_in_proj",
    )(x, mod, gain, w, qg, kg, cos, sin)


def _group_softmax_pv(q_ref, o_ref, rows, k, v_aug):
    for i in range(GROUP):
        lanes = slice(i * HEAD_DIM, (i + 1) * HEAD_DIM)
        s = _dot_nt(q_ref[rows, lanes], k)
        p = jnp.exp2(s - jnp.max(s, axis=-1, keepdims=True)).astype(BF16)
        o = _dot(p, v_aug)
        o_ref[rows, lanes] = (o[:, :HEAD_DIM] / o[:, HEAD_DIM:HEAD_DIM + 1]).astype(BF16)


def _ones_column(rows):
    lane = lax.broadcasted_iota(jnp.int32, (rows, HEAD_DIM), 1)
    return jnp.where(lane == 0, 1.0, 0.0).astype(BF16)


def _attn_kernel(q_ref, k_ref, v_ref, kc_ref, vc_ref, o_ref, kall_ref, vall_ref):
    g = pl.program_id(0)
    t = pl.program_id(2)
    T = q_ref.shape[0]
    S = k_ref.shape[0]

    @pl.when(g == 0)
    def _():
        for j in range(T // CTX_SEQ):
            keys = pl.ds(pl.multiple_of(t * T + j * CTX_SEQ, CTX_SEQ), CTX_SEQ)
            v_aug = jnp.concatenate([v_ref[keys, :], _ones_column(CTX_SEQ)], axis=-1)
            _group_softmax_pv(q_ref, o_ref, slice(j * CTX_SEQ, (j + 1) * CTX_SEQ),
                              k_ref[keys, :], v_aug)

    @pl.when((g > 0) & (t == 0))
    def _():
        kall_ref[0:S, :] = k_ref[...]
        kall_ref[S:, :] = kc_ref[...]
        vall_ref[0:S, 0:HEAD_DIM] = v_ref[...]
        vall_ref[S:, 0:HEAD_DIM] = vc_ref[...]
        vall_ref[:, HEAD_DIM:] = _ones_column(vall_ref.shape[0])

    @pl.when(g > 0)
    def _():
        for r0 in range(0, T, ATTN_CHAIN_ROWS):
            _group_softmax_pv(q_ref, o_ref, slice(r0, r0 + ATTN_CHAIN_ROWS),
                              kall_ref[...], vall_ref[...])


def _attention(q, k, v, cache_k, cache_v):
    T = ATTN_TILE
    assert T % CTX_SEQ == 0
    gw = GROUP * HEAD_DIM
    qspec = pl.BlockSpec((None, T, gw), lambda g, h, t: (g, t, h))
    kspec = pl.BlockSpec((None, GROUP_TOKENS, HEAD_DIM), lambda g, h, t: (g, 0, h))
    cspec = pl.BlockSpec((None, PAST_LEN, HEAD_DIM),
                         lambda g, h, t: (jnp.maximum(g - 1, 0), 0, h))
    n_keys = GROUP_TOKENS + PAST_LEN
    return pl.pallas_call(
        _attn_kernel,
        grid=(N_GROUPS, N_KV, GROUP_TOKENS // T),
        in_specs=[qspec, kspec, kspec, cspec, cspec],
        out_specs=qspec,
        out_shape=jax.ShapeDtypeStruct(q.shape, BF16),
        scratch_shapes=[pltpu.VMEM((n_keys, HEAD_DIM), BF16),
                        pltpu.VMEM((n_keys, 2 * HEAD_DIM), BF16)],
        compiler_params=_ARB3,
        name="gqa_attention",
    )(q, k, v, cache_k, cache_v)


def _rope_tables():
    rows = GROUP_TOKENS // GRID_W
    assert rows == GRID_W
    n_freq = HEAD_DIM // 4
    inv = 1.0 / (ROPE_THETA ** (jnp.arange(n_freq, dtype=F32) / n_freq))
    ang = jnp.arange(GRID_W).astype(F32)[:, None] * inv
    by_row = lambda a: jnp.repeat(a, GRID_W, axis=0)
    by_col = lambda a: jnp.tile(a, (rows, 1))
    cos_r, cos_c = by_row(jnp.cos(ang)), by_col(jnp.cos(ang))
    sin_r, sin_c = by_row(jnp.sin(ang)), by_col(jnp.sin(ang))
    cos = jnp.concatenate([cos_r, cos_c, cos_r, cos_c], axis=-1)
    sin = jnp.concatenate([-sin_r, -sin_c, sin_r, sin_c], axis=-1)
    return cos, sin


def kernel(x_prompt, x_sample, c, state_mlstm_C, state_mlstm_n, state_mlstm_m, cache_k, cache_v,
           c_ctx, w_mod, b_mod, norm1_g, norm2_g, w_in_ab, b_gate_ab, conv_w, conv_b, conv_ln_g,
           conv_ln_b, mlstm_norm_g, w_out_ab, w_in_c, q_norm_g, k_norm_g, w_out_c, w_ffn_in,
           w_ffn_out, final_norm_g):
    n_lat = x_sample.shape[0]
    n_ctx = x_prompt.shape[0]
    xp = x_prompt.reshape(1, GROUP_TOKENS, D_MODEL)

    cond8 = jnp.concatenate([c_ctx[None, :], c, jnp.zeros((8 - 1 - n_lat, D_MODEL), F32)], axis=0)
    mods = _adaln_all(cond8, w_mod, b_mod)[:, :N_GROUPS].reshape(2, N_GROUPS, 6, D_MODEL)
    row = lambda a: a.reshape(1, -1)
    w_ffn_in_b = w_ffn_in.astype(BF16)
    w_ffn_out_b = w_ffn_out.astype(BF16)

    main_cols = 2 * CONV_CH + 4 * M_WIDTH
    w_ab = w_in_ab[0]
    w_main = w_ab[:, :main_cols].astype(BF16)
    q_cols = slice(2 * CONV_CH, 2 * CONV_CH + M_WIDTH)
    v_cols = slice(2 * CONV_CH + 2 * M_WIDTH, 2 * CONV_CH + 3 * M_WIDTH)
    wqvt = jnp.concatenate([w_ab[:, q_cols], w_ab[:, v_cols]], axis=1).T.astype(BF16)
    gate_w = w_ab[:, main_cols:].reshape(D_MODEL, 2, 2, M_HEADS)
    gate_b = b_gate_ab[0].reshape(2, 2, M_HEADS)
    pad = 128 - N_STREAMS
    lanes = lambda a: jnp.pad(a.reshape(-1, N_STREAMS), ((0, 0), (0, pad)))
    wg = jnp.concatenate([lanes(gate_w[:, :, 0]), lanes(gate_w[:, :, 1])], axis=1).astype(BF16)
    bg = jnp.concatenate([lanes(gate_b[:, 0]), lanes(gate_b[:, 1])], axis=1)
    wgft = gate_w[:, :, 1].reshape(D_MODEL, N_STREAMS).T.astype(BF16)
    bgft = gate_b[:, 1].reshape(N_STREAMS, 1)
    u, qt, k, vt, o, gi, gf, gft = _ab_in(xp, x_sample, mods[0], row(norm1_g[0]), w_main, wqvt,
                                          wg, wgft, bg, bgft)

    c0 = state_mlstm_C[:, 0].reshape(n_lat, N_STREAMS, M_DK, M_DK)
    n0 = state_mlstm_n[:, 0].reshape(n_lat, N_STREAMS, M_DK)
    m0 = jnp.pad(state_mlstm_m[:, 0].reshape(n_lat, N_STREAMS), ((0, 0), (0, pad)))
    conv_w_rep = jnp.broadcast_to(conv_w[0][:, None, :], (CONV_WIDTH, SUBLANES, CONV_CH))
    conv_b_rep = jnp.broadcast_to(conv_b[0][None, :], (SUBLANES, CONV_CH))
    hf, hb, new_c, new_n, new_m, u_conv = _mlstm(
        qt, k, vt, gi, gf, gft, c0, n0, m0.reshape(n_lat, 1, 128),
        u, conv_w_rep, conv_b_rep, row(conv_ln_g[0]), row(conv_ln_b[0]))

    w_o = w_out_ab[0].astype(BF16)
    x = _l0_tail(u_conv, hf, hb, o, xp, x_sample, mods[0], row(mlstm_norm_g[0]),
                 w_o[:CONV_CH], w_o[CONV_CH:], row(norm2_g[0]),
                 w_ffn_in_b, w_ffn_out_b, 0)

    cos, sin = _rope_tables()
    kvw = N_KV * HEAD_DIM
    qk_cols = (N_Q + N_KV) * HEAD_DIM
    w_c = jnp.concatenate([_paired_layout(w_in_c[0][:, :qk_cols]), w_in_c[0][:, qk_cols:]],
                          axis=1).astype(BF16)
    qa, ka, va, k_norm, v_raw = _c_in(x, mods[1], row(norm1_g[1]), w_c,
                                      _paired_layout(row(q_norm_g[0])),
                                      _paired_layout(row(k_norm_g[0])), cos, sin)
    k_norm = _paired_layout(k_norm)
    ck = _paired_layout(cache_k[:, 0].reshape(n_lat, PAST_LEN, kvw)).astype(BF16)
    cv = cache_v[:, 0].reshape(n_lat, PAST_LEN, kvw).astype(BF16)
    oa = _attention(qa, ka, va, ck, cv)
    yp, y_sample = _l1_tail(oa, x, mods[1], w_out_c[0].astype(BF16), row(norm2_g[1]),
                            w_ffn_in_b, w_ffn_out_b, 1, row(final_norm_g))

    y_prompt = yp.reshape(n_ctx, CTX_SEQ, D_MODEL)
    new_state_c = new_c.reshape(n_ctx, 1, 2, M_HEADS, M_DK, M_DK)
    new_state_n = new_n.reshape(n_ctx, 1, 2, M_HEADS, M_DK)
    new_state_m = new_m[:, 0, :N_STREAMS].reshape(n_ctx, 1, 2, M_HEADS)
    new_k = k_norm.reshape(n_ctx, 1, CTX_SEQ, N_KV, HEAD_DIM)
    new_v = v_raw.reshape(n_ctx, 1, CTX_SEQ, N_KV, HEAD_DIM)
    return (y_prompt, y_sample, new_state_c, new_state_n, new_state_m, new_k, new_v)
```

```python
import jax
import jax.numpy as jnp
from jax import lax
from jax.experimental import pallas as pl
from jax.experimental.pallas import tpu as pltpu

F32 = jnp.float32
BF16 = jnp.bfloat16

D_MODEL = 1024
N_GROUPS = 5
GROUP_TOKENS = 4096
CTX_SEQ = 256
CONV_CH = 512
CONV_WIDTH = 31
M_HEADS = 4
M_DK = 128
M_WIDTH = 512
N_STREAMS = 2 * M_HEADS
STATE_ROWS = M_DK + 16
HEAD_DIM = 128
N_Q = 8
N_KV = 2
GROUP = N_Q // N_KV
PAST_LEN = 256
GRID_W = 64
ROPE_THETA = 10000.0
D_FF = 2816
EPS = 1e-6
LOG2_E = 1.4426950408889634

CHUNK = 256
TOKEN_TILE = 512
ROPE_TILE = 256
ATTN_TILE = 2048
ATTN_CHAIN_ROWS = 128
FFN_CUTS = (0, 1536, D_FF)
HALO = 16
SUBLANES = 8
VMEM_LIMIT = 56 * 1024 * 1024

_ARB2 = pltpu.CompilerParams(dimension_semantics=("arbitrary", "arbitrary"),
                             vmem_limit_bytes=VMEM_LIMIT)
_ARB3 = pltpu.CompilerParams(dimension_semantics=("arbitrary", "arbitrary", "arbitrary"),
                             vmem_limit_bytes=VMEM_LIMIT)


def _rms(x):
    return x * lax.rsqrt(jnp.mean(x * x, axis=-1, keepdims=True) + EPS)


def _modulated(x, gain, shift, scale):
    return (_rms(x) * gain) * (1.0 + scale) + shift


def _sigmoid(x):
    return 1.0 / (1.0 + jnp.exp(-x))


def _log_sigmoid(x):
    return jnp.minimum(x, 0.0) - jnp.log1p(jnp.exp(-jnp.abs(x)))


def _dot(a, b):
    return jnp.dot(a, b, preferred_element_type=F32)


def _dot_nt(a, b):
    return lax.dot_general(a, b, (((1,), (1,)), ((), ())), preferred_element_type=F32)


def _prefix_sum(x, axis):
    pos = lax.broadcasted_iota(jnp.int32, x.shape, axis)
    k = 1
    while k < x.shape[axis]:
        x = x + jnp.where(pos >= k, pltpu.roll(x, k, axis), 0.0)
        k *= 2
    return x


def _resident_spec(a):
    return pl.BlockSpec(a.shape, lambda *_: (0,) * a.ndim, pipeline_mode=pl.Buffered(1))


def _resident_layer_spec(a, layer):
    return pl.BlockSpec((None,) + a.shape[1:], lambda *_: (layer,) + (0,) * (a.ndim - 1),
                        pipeline_mode=pl.Buffered(1))


def _ctx_spec(tile, nt, width):
    return pl.BlockSpec((None, tile, width), lambda g, t: (0, jnp.where(g == 0, t, nt - 1), 0))


def _lat_spec(tile, width):
    return pl.BlockSpec((None, tile, width),
                        lambda g, t: (jnp.maximum(g - 1, 0), jnp.where(g > 0, t, 0), 0))


def _mod_kernel(cond_ref, w_ref, b_ref, o_ref):
    c = cond_ref[...]
    s = (c * _sigmoid(c)).astype(BF16)
    o_ref[...] = _dot(s, w_ref[...].astype(BF16)) + b_ref[...]


def _adaln_all(cond8, w_mod, b_mod):
    depth, d, n = w_mod.shape
    tn = 1024
    return pl.pallas_call(
        _mod_kernel,
        grid=(depth, n // tn),
        in_specs=[
            pl.BlockSpec((8, d), lambda l, j: (0, 0)),
            pl.BlockSpec((None, d, tn), lambda l, j: (l, 0, j)),
            pl.BlockSpec((None, 1, tn), lambda l, j: (l, 0, j)),
        ],
        out_specs=pl.BlockSpec((None, 8, tn), lambda l, j: (l, 0, j)),
        out_shape=jax.ShapeDtypeStruct((depth, 8, n), F32),
        compiler_params=_ARB2,
        name="adaln_mod",
    )(cond8, w_mod, b_mod.reshape(depth, 1, n))


def _ab_in_kernel(xp_ref, xs_ref, mod_ref, g_ref, w_ref, wqvt_ref,
                  wg_ref, wgft_ref, bg_ref, bgft_ref,
                  u_ref, qt_ref, k_ref, vt_ref, o_ref, gi_ref, gf_ref, gft_ref):
    x = jnp.where(pl.program_id(0) == 0, xp_ref[...], xs_ref[...])
    h = _modulated(x, g_ref[...], mod_ref[0:1, :], mod_ref[1:2, :])
    hb = h.astype(BF16)

    def proj(c):
        return _dot(hb, w_ref[:, c * CONV_CH:(c + 1) * CONV_CH])

    u_ref[...] = proj(0) * _sigmoid(proj(1))
    k_ref[...] = proj(3) * (M_DK ** -0.5)
    o_ref[...] = proj(5)
    qvt = _dot_nt(wqvt_ref[...], hb).astype(BF16)
    qt_ref[...] = qvt[:M_WIDTH]
    vt_ref[...] = qvt[M_WIDTH:]
    lane = lax.broadcasted_iota(jnp.int32, (x.shape[0], 128), 1)
    gates = _dot(hb, wg_ref[...]) + bg_ref[...]
    gi_ref[...] = gates[:, :128]
    gf_ref[...] = jnp.where(lane < N_STREAMS, _log_sigmoid(gates[:, 128:]), 0.0)
    gft_ref[...] = _log_sigmoid(_dot_nt(wgft_ref[...], hb) + bgft_ref[...])


def _ab_in(xp, xs, mod, gain, w_main, wqvt, wg, wgft, bg, bgft):
    tm = TOKEN_TILE
    nt = GROUP_TOKENS // tm
    tok = lambda width: pl.BlockSpec((None, tm, width), lambda g, t: (g, t, 0))
    feat = lambda rows: pl.BlockSpec((None, rows, tm), lambda g, t: (g, 0, t))
    const = _resident_spec
    tok_sds = lambda width: jax.ShapeDtypeStruct((N_GROUPS, GROUP_TOKENS, width), F32)
    feat_sds = lambda rows, dt: jax.ShapeDtypeStruct((N_GROUPS, rows, GROUP_TOKENS), dt)
    consts = (gain, w_main, wqvt, wg, wgft, bg, bgft)
    return pl.pallas_call(
        _ab_in_kernel,
        grid=(N_GROUPS, nt),
        in_specs=[_ctx_spec(tm, nt, D_MODEL), _lat_spec(tm, D_MODEL),
                  pl.BlockSpec((None, 6, D_MODEL), lambda g, t: (g, 0, 0))]
                 + [const(a) for a in consts],
        out_specs=[tok(CONV_CH), feat(M_WIDTH), tok(M_WIDTH), feat(M_WIDTH), tok(M_WIDTH),
                   tok(128), tok(128), feat(N_STREAMS)],
        out_shape=[tok_sds(CONV_CH), feat_sds(M_WIDTH, BF16), tok_sds(M_WIDTH),
                   feat_sds(M_WIDTH, BF16), tok_sds(M_WIDTH),
                   tok_sds(128), tok_sds(128), feat_sds(N_STREAMS, F32)],
        compiler_params=_ARB2,
        name="ab_in_proj",
    )(xp, xs, mod, *consts)


def _mlstm_direction(qt_ref, k_ref, vt_ref, gi_ref, gf_ref, gft_ref, h_ref, cn_s, m_s, d,
                     side_jobs):
    L = k_ref.shape[0]
    r = lax.broadcasted_iota(jnp.int32, (L, L), 0)
    c = lax.broadcasted_iota(jnp.int32, (L, L), 1)
    le = r <= c
    ge = r >= c
    gf = gf_ref[...]
    gft = gft_ref[...]
    p_col = _prefix_sum(gf, 0)
    p_row = _prefix_sum(gft, 1)
    tot_col = p_col[L - 1:L, :]
    if d == 0:
        b_col, b_row, mask = p_col, p_row, le
    else:
        b_col = tot_col - p_col + gf
        b_row = p_row[:, L - 1:L] - p_row + gft
        mask = ge
    c_all = gi_ref[...] - b_col
    m_prev_all = m_s[...]
    m_fin = jnp.maximum(m_prev_all, jnp.max(c_all, axis=0, keepdims=True))
    w_all = jnp.exp(c_all - m_fin)
    a_all = jnp.exp(m_prev_all - m_fin)
    lane = lax.broadcasted_iota(jnp.int32, (1, 128), 1)
    mine = (lane >= d * M_HEADS) & (lane < (d + 1) * M_HEADS)
    m_s[...] = jnp.where(mine, tot_col + m_fin, m_prev_all)
    ones_row = jnp.where(lax.broadcasted_iota(jnp.int32, (STATE_ROWS - M_DK, L), 0) == 0,
                         1.0, 0.0).astype(BF16)
    for h in range(M_HEADS):
        s_idx = d * M_HEADS + h
        sl = slice(h * M_DK, (h + 1) * M_DK)
        c_col = c_all[:, s_idx:s_idx + 1]
        m_prev = m_prev_all[:, s_idx:s_idx + 1]
        cm = jnp.where(mask, c_col, -jnp.inf)
        m_row = jnp.maximum(jnp.max(cm, axis=0, keepdims=True), m_prev)
        dmat = jnp.exp(cm - m_row)
        k = k_ref[:, sl]
        cn = cn_s[s_idx]
        lhs = jnp.concatenate([k.astype(BF16), cn.astype(BF16)], axis=0)
        r1 = _dot(lhs, qt_ref[sl, :])
        st = (r1[0:L] * dmat).astype(BF16)
        vt_aug = jnp.concatenate([vt_ref[sl, :], ones_row], axis=0)
        w_inter = jnp.exp(m_prev - m_row)
        tot = _dot(vt_aug, st) + w_inter * r1[L:L + STATE_ROWS]
        den = tot[M_DK:M_DK + 1, :]
        m_t = b_row[s_idx:s_idx + 1, :] + m_row
        ht = tot[0:M_DK, :] / jnp.maximum(jnp.abs(den), jnp.exp(-m_t))
        h_ref[:, sl] = ht.T
        kw = (k * w_all[:, s_idx:s_idx + 1]).astype(BF16)
        cn_s[s_idx] = a_all[:, s_idx:s_idx + 1] * cn + _dot(vt_aug, kw)
        side_jobs.pop(0)()


def _mlstm_kernel(qtf_ref, kf_ref, vtf_ref, gif_ref, gff_ref, gftf_ref,
                  qtb_ref, kb_ref, vtb_ref, gib_ref, gfb_ref, gftb_ref,
                  c0_ref, n0_ref, m0_ref,
                  up_ref, uc_ref, un_ref, cw_ref, cb_ref, lg_ref, lb_ref,
                  hf_ref, hb_ref, co_ref, no_ref, mo_ref, uconv_ref,
                  cn_s, m_s, ext_ref):
    g = pl.program_id(0)
    c = pl.program_id(1)


    @pl.when(g == 0)
    def _():
        cn_s[...] = jnp.zeros_like(cn_s)
        m_s[...] = jnp.zeros_like(m_s)

    @pl.when((g > 0) & (c == 0))
    def _():
        cn_s[...] = jnp.zeros_like(cn_s)
        for s in range(N_STREAMS):
            cn_s[s, 0:M_DK, :] = c0_ref[s].T
            cn_s[s, M_DK:M_DK + 1, :] = n0_ref[s:s + 1, :]
        m_s[...] = m0_ref[...]

    jobs = _conv_jobs(g, c, pl.num_programs(1), up_ref, uc_ref, un_ref, cw_ref, cb_ref,
                      lg_ref, lb_ref, ext_ref, uconv_ref, N_STREAMS)
    jobs.pop(0)()
    _mlstm_direction(qtf_ref, kf_ref, vtf_ref, gif_ref, gff_ref, gftf_ref, hf_ref, cn_s, m_s, 0,
                     jobs)
    _mlstm_direction(qtb_ref, kb_ref, vtb_ref, gib_ref, gfb_ref, gftb_ref, hb_ref, cn_s, m_s, 1,
                     jobs)

    @pl.when(g == 0)
    def _():
        for s in range(N_STREAMS):
            co_ref[s] = cn_s[s, 0:M_DK, :].T
            no_ref[s:s + 1, :] = cn_s[s, M_DK:M_DK + 1, :]
        mo_ref[...] = m_s[...]


def _mlstm(qt, k, vt, gi, gf, gft, c0, n0, m0, u, conv_w, conv_b, ln_g, ln_b):
    L = CHUNK
    nc = GROUP_TOKENS // L
    n_ctx = GROUP_TOKENS // CTX_SEQ
    assert CTX_SEQ == L and n_ctx == nc
    per = L // HALO
    nh = GROUP_TOKENS // HALO
    conv_consts = (conv_w, conv_b, ln_g, ln_b)
    fwd = lambda g, c: c
    bwd = lambda g, c: jnp.where(g == 0, c, nc - 1 - c)
    tok = lambda width, pos: pl.BlockSpec((None, L, width), lambda g, c: (g, pos(g, c), 0))
    feat = lambda rows, pos: pl.BlockSpec((None, rows, L), lambda g, c: (g, 0, pos(g, c)))
    lat = lambda g, c: jnp.maximum(g - 1, 0)
    ctx = lambda g, c: jnp.where(g == 0, c, n_ctx - 1)
    per_dir = lambda pos: [feat(M_WIDTH, pos), tok(M_WIDTH, pos), feat(M_WIDTH, pos),
                           tok(128, pos), tok(128, pos), feat(N_STREAMS, pos)]
    return pl.pallas_call(
        _mlstm_kernel,
        grid=(N_GROUPS, nc),
        in_specs=per_dir(fwd) + per_dir(bwd) + [
            pl.BlockSpec((None, N_STREAMS, M_DK, M_DK), lambda g, c: (lat(g, c), 0, 0, 0)),
            pl.BlockSpec((None, N_STREAMS, M_DK), lambda g, c: (lat(g, c), 0, 0)),
            pl.BlockSpec((None, 1, 128), lambda g, c: (lat(g, c), 0, 0)),
            pl.BlockSpec((None, HALO, CONV_CH),
                         lambda g, c: (g, jnp.maximum(c * per - 1, 0), 0)),
            tok(CONV_CH, fwd),
            pl.BlockSpec((None, HALO, CONV_CH),
                         lambda g, c: (g, jnp.minimum((c + 1) * per, nh - 1), 0)),
        ] + [_resident_spec(a) for a in conv_consts],
        out_specs=[
            tok(M_WIDTH, fwd), tok(M_WIDTH, bwd),
            pl.BlockSpec((None, N_STREAMS, M_DK, M_DK), lambda g, c: (ctx(g, c), 0, 0, 0)),
            pl.BlockSpec((None, N_STREAMS, M_DK), lambda g, c: (ctx(g, c), 0, 0)),
            pl.BlockSpec((None, 1, 128), lambda g, c: (ctx(g, c), 0, 0)),
            tok(CONV_CH, fwd),
        ],
        out_shape=[
            jax.ShapeDtypeStruct((N_GROUPS, GROUP_TOKENS, M_WIDTH), F32),
            jax.ShapeDtypeStruct((N_GROUPS, GROUP_TOKENS, M_WIDTH), F32),
            jax.ShapeDtypeStruct((n_ctx, N_STREAMS, M_DK, M_DK), F32),
            jax.ShapeDtypeStruct((n_ctx, N_STREAMS, M_DK), F32),
            jax.ShapeDtypeStruct((n_ctx, 1, 128), F32),
            jax.ShapeDtypeStruct((N_GROUPS, GROUP_TOKENS, CONV_CH), BF16),
        ],
        scratch_shapes=[
            pltpu.VMEM((N_STREAMS, STATE_ROWS, M_DK), F32),
            pltpu.VMEM((1, 128), F32),
            pltpu.VMEM((SUBLANES, L + 2 * HALO, CONV_CH), F32),
        ],
        compiler_params=_ARB2,
        name="mlstm_scan",
    )(qt, k, vt, gi, gf, gft, qt, k, vt, gi, gf, gft, c0, n0, m0, u, u, u, *conv_consts)


def _conv_jobs(g, t, nt, up_ref, uc_ref, un_ref, cw_ref, cb_ref, lg_ref, lb_ref,
               ext_ref, out_ref, n_blocks):
    T = uc_ref.shape[0]
    rows = T // n_blocks
    span = T + 2 * HALO - SUBLANES
    off = HALO - CONV_WIDTH // 2

    def prepare():
        prev_on = jnp.where((g > 0) & (t > 0), 1.0, 0.0).astype(F32)
        next_on = jnp.where((g > 0) & (t < nt - 1), 1.0, 0.0).astype(F32)
        ext_ref[0, 0:HALO, :] = up_ref[...] * prev_on
        ext_ref[0, HALO:HALO + T, :] = uc_ref[...]
        ext_ref[0, HALO + T:HALO + T + HALO, :] = un_ref[...] * next_on
        for s in range(1, SUBLANES):
            ext_ref[s, 0:span, :] = ext_ref[0, s:s + span, :]

    def block(r0):
        def run():
            groups = rows // SUBLANES
            accs = [cb_ref[...]] * groups
            for j in range(CONV_WIDTH):
                phase, base = (off + j) % SUBLANES, (off + j) // SUBLANES * SUBLANES
                w = cw_ref[j]
                for q in range(groups):
                    lo = r0 + base + q * SUBLANES
                    accs[q] = accs[q] + w * ext_ref[phase, lo:lo + SUBLANES, :]
            y = jnp.concatenate(accs, axis=0)
            mu = jnp.mean(y, axis=-1, keepdims=True)
            yc = y - mu
            var = jnp.mean(yc * yc, axis=-1, keepdims=True)
            y = yc * lax.rsqrt(var + EPS) * lg_ref[...] + lb_ref[...]
            out_ref[r0:r0 + rows, :] = (y * _sigmoid(y)).astype(BF16)
        return run

    return [prepare] + [block(r0) for r0 in range(0, T, rows)]


def _mix_stage(hf_ref, hb_ref, o_ref, mg_ref):
    hm = hf_ref[...] + hb_ref[...]
    hm = jnp.concatenate(
        [_rms(hm[:, h * M_DK:(h + 1) * M_DK]) for h in range(M_HEADS)], axis=-1)
    return (_sigmoid(o_ref[...]) * (hm * mg_ref[...])).astype(BF16)


def _ffn_tail(x, mod_ref, g_ref, win_ref, wout_ref):
    hb = _modulated(x, g_ref[...], mod_ref[3:4, :], mod_ref[4:5, :]).astype(BF16)
    acc = jnp.zeros(x.shape, F32)
    for lo, hi in zip(FFN_CUTS[:-1], FFN_CUTS[1:]):
        gt = _dot(hb, win_ref[:, lo:hi])
        up = _dot(hb, win_ref[:, D_FF + lo:D_FF + hi])
        act = (gt * _sigmoid(gt) * up).astype(BF16)
        acc = acc + _dot(act, wout_ref[lo:hi, :])
    return x + mod_ref[5:6, :] * acc


def _l0_tail_kernel(u_ref, hf_ref, hb_ref, o_ref, xp_ref, xs_ref, mod_ref,
                    mg_ref, wu_ref, wm_ref, g2_ref, win_ref, wout_ref, out_ref):
    hm = _mix_stage(hf_ref, hb_ref, o_ref, mg_ref)
    x = jnp.where(pl.program_id(0) == 0, xp_ref[...], xs_ref[...])
    out = _dot(u_ref[...], wu_ref[...]) + _dot(hm, wm_ref[...])
    out_ref[...] = _ffn_tail(x + mod_ref[2:3, :] * out, mod_ref, g2_ref, win_ref, wout_ref)


def _l0_tail(u_conv, hf, hb, o, xp, xs, mod, mnorm_g, w_u, w_m, gain2, w_in_all, w_out_all,
             layer):
    T = TOKEN_TILE
    nt = GROUP_TOKENS // T
    tok = lambda width: pl.BlockSpec((None, T, width), lambda g, t: (g, t, 0))
    consts = (mnorm_g, w_u, w_m, gain2)
    stacked = (w_in_all, w_out_all)
    return pl.pallas_call(
        _l0_tail_kernel,
        grid=(N_GROUPS, nt),
        in_specs=[
            tok(CONV_CH), tok(M_WIDTH), tok(M_WIDTH), tok(M_WIDTH),
            _ctx_spec(T, nt, D_MODEL), _lat_spec(T, D_MODEL),
            pl.BlockSpec((None, 6, D_MODEL), lambda g, t: (g, 0, 0)),
        ] + [_resident_spec(a) for a in consts]
          + [_resident_layer_spec(a, layer) for a in stacked],
        out_specs=tok(D_MODEL),
        out_shape=jax.ShapeDtypeStruct((N_GROUPS, GROUP_TOKENS, D_MODEL), F32),
        compiler_params=_ARB2,
        name="l0_mix_ffn",
    )(u_conv, hf, hb, o, xp, xs, mod, *consts, *stacked)


def _l1_tail_kernel(o_ref, x_ref, mod_ref, wc_ref, g2_ref, win_ref, wout_ref, fg_ref,
                    yp_ref, ys_ref):
    x = x_ref[...] + mod_ref[2:3, :] * _dot(o_ref[...], wc_ref[...])
    y = _rms(_ffn_tail(x, mod_ref, g2_ref, win_ref, wout_ref)) * fg_ref[...]
    g = pl.program_id(0)

    @pl.when(g == 0)
    def _():
        yp_ref[...] = y

    @pl.when(g > 0)
    def _():
        ys_ref[...] = y


def _l1_tail(o, x, mod, w_c, gain2, w_in_all, w_out_all, layer, final_gain):
    tm = TOKEN_TILE
    nt = GROUP_TOKENS // tm
    tok = pl.BlockSpec((None, tm, D_MODEL), lambda g, t: (g, t, 0))
    consts = (w_c, gain2, w_in_all, w_out_all, final_gain)
    const_specs = [_resident_spec(w_c), _resident_spec(gain2),
                   _resident_layer_spec(w_in_all, layer), _resident_layer_spec(w_out_all, layer),
                   _resident_spec(final_gain)]
    return pl.pallas_call(
        _l1_tail_kernel,
        grid=(N_GROUPS, nt),
        in_specs=[tok, tok, pl.BlockSpec((None, 6, D_MODEL), lambda g, t: (g, 0, 0))]
                 + const_specs,
        out_specs=[_ctx_spec(tm, nt, D_MODEL), _lat_spec(tm, D_MODEL)],
        out_shape=[jax.ShapeDtypeStruct((1, GROUP_TOKENS, D_MODEL), F32),
                   jax.ShapeDtypeStruct((N_GROUPS - 1, GROUP_TOKENS, D_MODEL), F32)],
        compiler_params=_ARB2,
        name="l1_proj_ffn",
    )(o, x, mod, *consts)


def _paired_layout(a):
    quarter = HEAD_DIM // 4
    b = a.reshape(a.shape[:-1] + (a.shape[-1] // HEAD_DIM, 4, quarter))
    b = jnp.stack([b[..., 0, :], b[..., 2, :], b[..., 1, :], b[..., 3, :]], axis=-2)
    return b.reshape(a.shape)


def _rope(x, cos, sin):
    return x * cos + pltpu.roll(x, HEAD_DIM // 2, 1) * sin


def _c_in_kernel(x_ref, mod_ref, g_ref, w_ref, qg_ref, kg_ref, cos_ref, sin_ref,
                 q_ref, k_ref, v_ref, kn_ref, vn_ref):
    hb = _modulated(x_ref[...], g_ref[...], mod_ref[0:1, :], mod_ref[1:2, :]).astype(BF16)
    latent = pl.program_id(0) > 0
    cos = jnp.where(latent, cos_ref[...], 1.0)
    sin = jnp.where(latent, sin_ref[...], 0.0)
    q = _dot(hb, w_ref[:, 0:N_Q * HEAD_DIM])
    qg = qg_ref[...] * (HEAD_DIM ** -0.5 * LOG2_E)
    for h in range(N_Q):
        lanes = slice(h * HEAD_DIM, (h + 1) * HEAD_DIM)
        q_ref[:, lanes] = _rope(_rms(q[:, lanes]) * qg, cos, sin).astype(BF16)
    kv = _dot(hb, w_ref[:, N_Q * HEAD_DIM:(N_Q + 2 * N_KV) * HEAD_DIM])
    kg = kg_ref[...]
    kn = [_rms(kv[:, h * HEAD_DIM:(h + 1) * HEAD_DIM]) * kg for h in range(N_KV)]
    for h in range(N_KV):
        k_ref[:, h * HEAD_DIM:(h + 1) * HEAD_DIM] = _rope(kn[h], cos, sin).astype(BF16)
    v = kv[:, N_KV * HEAD_DIM:]
    v_ref[...] = v.astype(BF16)

    @pl.when(pl.program_id(0) == 0)
    def _():
        for h in range(N_KV):
            kn_ref[:, h * HEAD_DIM:(h + 1) * HEAD_DIM] = kn[h]
        vn_ref[...] = v


def _c_in(x, mod, gain, w, qg, kg, cos, sin):
    tm = ROPE_TILE
    kvw = N_KV * HEAD_DIM
    tok = lambda width: pl.BlockSpec((None, tm, width), lambda g, t: (g, t, 0))
    const = _resident_spec
    table = pl.BlockSpec((tm, HEAD_DIM), lambda g, t: (t, 0))
    nt = GROUP_TOKENS // tm
    ctx_only = pl.BlockSpec((tm, kvw), lambda g, t: (jnp.where(g == 0, t, nt - 1), 0))
    sds = lambda width, dt: jax.ShapeDtypeStruct((N_GROUPS, GROUP_TOKENS, width), dt)
    return pl.pallas_call(
        _c_in_kernel,
        grid=(N_GROUPS, GROUP_TOKENS // tm),
        in_specs=[tok(D_MODEL), pl.BlockSpec((None, 6, D_MODEL), lambda g, t: (g, 0, 0)),
                  const(gain), const(w), const(qg), const(kg), table, table],
        out_specs=[tok(N_Q * HEAD_DIM), tok(kvw), tok(kvw), ctx_only, ctx_only],
        out_shape=[sds(N_Q * HEAD_DIM, BF16), sds(kvw, BF16), sds(kvw, BF16),
                   jax.ShapeDtypeStruct((GROUP_TOKENS, kvw), F32),
                   jax.ShapeDtypeStruct((GROUP_TOKENS, kvw), F32)],
        compiler_params=_ARB2,
        name="attn_in_proj",
    )(x, mod, gain, w, qg, kg, cos, sin)


def _group_softmax_pv(q_ref, o_ref, rows, k, v_aug):
    for i in range(GROUP):
        lanes = slice(i * HEAD_DIM, (i + 1) * HEAD_DIM)
        s = _dot_nt(q_ref[rows, lanes], k)
        p = jnp.exp2(s - jnp.max(s, axis=-1, keepdims=True)).astype(BF16)
        o = _dot(p, v_aug)
        o_ref[rows, lanes] = (o[:, :HEAD_DIM] / o[:, HEAD_DIM:HEAD_DIM + 1]).astype(BF16)


def _ones_column(rows):
    lane = lax.broadcasted_iota(jnp.int32, (rows, HEAD_DIM), 1)
    return jnp.where(lane == 0, 1.0, 0.0).astype(BF16)


def _attn_kernel(q_ref, k_ref, v_ref, kc_ref, vc_ref, o_ref, kall_ref, vall_ref):
    g = pl.program_id(0)
    t = pl.program_id(2)
    T = q_ref.shape[0]
    S = k_ref.shape[0]

    @pl.when(g == 0)
    def _():
        for j in range(T // CTX_SEQ):
            keys = pl.ds(pl.multiple_of(t * T + j * CTX_SEQ, CTX_SEQ), CTX_SEQ)
            v_aug = jnp.concatenate([v_ref[keys, :], _ones_column(CTX_SEQ)], axis=-1)
            _group_softmax_pv(q_ref, o_ref, slice(j * CTX_SEQ, (j + 1) * CTX_SEQ),
                              k_ref[keys, :], v_aug)

    @pl.when((g > 0) & (t == 0))
    def _():
        kall_ref[0:S, :] = k_ref[...]
        kall_ref[S:, :] = kc_ref[...]
        vall_ref[0:S, 0:HEAD_DIM] = v_ref[...]
        vall_ref[S:, 0:HEAD_DIM] = vc_ref[...]
        vall_ref[:, HEAD_DIM:] = _ones_column(vall_ref.shape[0])

    @pl.when(g > 0)
    def _():
        for r0 in range(0, T, ATTN_CHAIN_ROWS):
            _group_softmax_pv(q_ref, o_ref, slice(r0, r0 + ATTN_CHAIN_ROWS),
                              kall_ref[...], vall_ref[...])


def _attention(q, k, v, cache_k, cache_v):
    T = ATTN_TILE
    assert T % CTX_SEQ == 0
    gw = GROUP * HEAD_DIM
    qspec = pl.BlockSpec((None, T, gw), lambda g, h, t: (g, t, h))
    kspec = pl.BlockSpec((None, GROUP_TOKENS, HEAD_DIM), lambda g, h, t: (g, 0, h))
    cspec = pl.BlockSpec((None, PAST_LEN, HEAD_DIM),
                         lambda g, h, t: (jnp.maximum(g - 1, 0), 0, h))
    n_keys = GROUP_TOKENS + PAST_LEN
    return pl.pallas_call(
        _attn_kernel,
        grid=(N_GROUPS, N_KV, GROUP_TOKENS // T),
        in_specs=[qspec, kspec, kspec, cspec, cspec],
        out_specs=qspec,
        out_shape=jax.ShapeDtypeStruct(q.shape, BF16),
        scratch_shapes=[pltpu.VMEM((n_keys, HEAD_DIM), BF16),
                        pltpu.VMEM((n_keys, 2 * HEAD_DIM), BF16)],
        compiler_params=_ARB3,
        name="gqa_attention",
    )(q, k, v, cache_k, cache_v)


def _rope_tables():
    rows = GROUP_TOKENS // GRID_W
    assert rows == GRID_W
    n_freq = HEAD_DIM // 4
    inv = 1.0 / (ROPE_THETA ** (jnp.arange(n_freq, dtype=F32) / n_freq))
    ang = jnp.arange(GRID_W).astype(F32)[:, None] * inv
    by_row = lambda a: jnp.repeat(a, GRID_W, axis=0)
    by_col = lambda a: jnp.tile(a, (rows, 1))
    cos_r, cos_c = by_row(jnp.cos(ang)), by_col(jnp.cos(ang))
    sin_r, sin_c = by_row(jnp.sin(ang)), by_col(jnp.sin(ang))
    cos = jnp.concatenate([cos_r, cos_c, cos_r, cos_c], axis=-1)
    sin = jnp.concatenate([-sin_r, -sin_c, sin_r, sin_c], axis=-1)
    return cos, sin


def kernel(x_prompt, x_sample, c, state_mlstm_C, state_mlstm_n, state_mlstm_m, cache_k, cache_v,
           c_ctx, w_mod, b_mod, norm1_g, norm2_g, w_in_ab, b_gate_ab, conv_w, conv_b, conv_ln_g,
           conv_ln_b, mlstm_norm_g, w_out_ab, w_in_c, q_norm_g, k_norm_g, w_out_c, w_ffn_in,
           w_ffn_out, final_norm_g):
    n_lat = x_sample.shape[0]
    n_ctx = x_prompt.shape[0]
    xp = x_prompt.reshape(1, GROUP_TOKENS, D_MODEL)

    cond8 = jnp.concatenate([c_ctx[None, :], c, jnp.zeros((8 - 1 - n_lat, D_MODEL), F32)], axis=0)
    mods = _adaln_all(cond8, w_mod, b_mod)[:, :N_GROUPS].reshape(2, N_GROUPS, 6, D_MODEL)
    row = lambda a: a.reshape(1, -1)
    w_ffn_in_b = w_ffn_in.astype(BF16)
    w_ffn_out_b = w_ffn_out.astype(BF16)

    main_cols = 2 * CONV_CH + 4 * M_WIDTH
    w_ab = w_in_ab[0]
    w_main = w_ab[:, :main_cols].astype(BF16)
    q_cols = slice(2 * CONV_CH, 2 * CONV_CH + M_WIDTH)
    v_cols = slice(2 * CONV_CH + 2 * M_WIDTH, 2 * CONV_CH + 3 * M_WIDTH)
    wqvt = jnp.concatenate([w_ab[:, q_cols], w_ab[:, v_cols]], axis=1).T.astype(BF16)
    gate_w = w_ab[:, main_cols:].reshape(D_MODEL, 2, 2, M_HEADS)
    gate_b = b_gate_ab[0].reshape(2, 2, M_HEADS)
    pad = 128 - N_STREAMS
    lanes = lambda a: jnp.pad(a.reshape(-1, N_STREAMS), ((0, 0), (0, pad)))
    wg = jnp.concatenate([lanes(gate_w[:, :, 0]), lanes(gate_w[:, :, 1])], axis=1).astype(BF16)
    bg = jnp.concatenate([lanes(gate_b[:, 0]), lanes(gate_b[:, 1])], axis=1)
    wgft = gate_w[:, :, 1].reshape(D_MODEL, N_STREAMS).T.astype(BF16)
    bgft = gate_b[:, 1].reshape(N_STREAMS, 1)
    u, qt, k, vt, o, gi, gf, gft = _ab_in(xp, x_sample, mods[0], row(norm1_g[0]), w_main, wqvt,
                                          wg, wgft, bg, bgft)

    c0 = state_mlstm_C[:, 0].reshape(n_lat, N_STREAMS, M_DK, M_DK)
    n0 = state_mlstm_n[:, 0].reshape(n_lat, N_STREAMS, M_DK)
    m0 = jnp.pad(state_mlstm_m[:, 0].reshape(n_lat, N_STREAMS), ((0, 0), (0, pad)))
    conv_w_rep = jnp.broadcast_to(conv_w[0][:, None, :], (CONV_WIDTH, SUBLANES, CONV_CH))
    conv_b_rep = jnp.broadcast_to(conv_b[0][None, :], (SUBLANES, CONV_CH))
    hf, hb, new_c, new_n, new_m, u_conv = _mlstm(
        qt, k, vt, gi, gf, gft, c0, n0, m0.reshape(n_lat, 1, 128),
        u, conv_w_rep, conv_b_rep, row(conv_ln_g[0]), row(conv_ln_b[0]))

    w_o = w_out_ab[0].astype(BF16)
    x = _l0_tail(u_conv, hf, hb, o, xp, x_sample, mods[0], row(mlstm_norm_g[0]),
                 w_o[:CONV_CH], w_o[CONV_CH:], row(norm2_g[0]),
                 w_ffn_in_b, w_ffn_out_b, 0)

    cos, sin = _rope_tables()
    kvw = N_KV * HEAD_DIM
    qk_cols = (N_Q + N_KV) * HEAD_DIM
    w_c = jnp.concatenate([_paired_layout(w_in_c[0][:, :qk_cols]), w_in_c[0][:, qk_cols:]],
                          axis=1).astype(BF16)
    qa, ka, va, k_norm, v_raw = _c_in(x, mods[1], row(norm1_g[1]), w_c,
                                      _paired_layout(row(q_norm_g[0])),
                                      _paired_layout(row(k_norm_g[0])), cos, sin)
    k_norm = _paired_layout(k_norm)
    ck = _paired_layout(cache_k[:, 0].reshape(n_lat, PAST_LEN, kvw)).astype(BF16)
    cv = cache_v[:, 0].reshape(n_lat, PAST_LEN, kvw).astype(BF16)
    oa = _attention(qa, ka, va, ck, cv)
    yp, y_sample = _l1_tail(oa, x, mods[1], w_out_c[0].astype(BF16), row(norm2_g[1]),
                            w_ffn_in_b, w_ffn_out_b, 1, row(final_norm_g))

    y_prompt = yp.reshape(n_ctx, CTX_SEQ, D_MODEL)
    new_state_c = new_c.reshape(n_ctx, 1, 2, M_HEADS, M_DK, M_DK)
    new_state_n = new_n.reshape(n_ctx, 1, 2, M_HEADS, M_DK)
    new_state_m = new_m[:, 0, :N_STREAMS].reshape(n_ctx, 1, 2, M_HEADS)
    new_k = k_norm.reshape(n_ctx, 1, CTX_SEQ, N_KV, HEAD_DIM)
    new_v = v_raw.reshape(n_ctx, 1, CTX_SEQ, N_KV, HEAD_DIM)
    return (y_prompt, y_sample, new_state_c, new_state_n, new_state_m, new_k, new_v)
```

```python
import jax
import jax.numpy as jnp
from jax import lax
from jax.experimental import pallas as pl
from jax.experimental.pallas import tpu as pltpu

F32 = jnp.float32
BF16 = jnp.bfloat16

D_MODEL = 1024
N_GROUPS = 5
GROUP_TOKENS = 4096
CTX_SEQ = 256
CONV_CH = 512
CONV_WIDTH = 31
M_HEADS = 4
M_DK = 128
M_WIDTH = 512
N_STREAMS = 2 * M_HEADS
STATE_ROWS = M_DK + 16
HEAD_DIM = 128
N_Q = 8
N_KV = 2
GROUP = N_Q // N_KV
PAST_LEN = 256
GRID_W = 64
ROPE_THETA = 10000.0
D_FF = 2816
EPS = 1e-6
LOG2_E = 1.4426950408889634

CHUNK = 256
TOKEN_TILE = 512
ROPE_TILE = 256
ATTN_TILE = 1024
ATTN_CHAIN_ROWS = 128
FFN_CUTS = (0, 1536, D_FF)
HALO = 16
SUBLANES = 8
VMEM_LIMIT = 56 * 1024 * 1024

_ARB2 = pltpu.CompilerParams(dimension_semantics=("arbitrary", "arbitrary"),
                             vmem_limit_bytes=VMEM_LIMIT)
_ARB3 = pltpu.CompilerParams(dimension_semantics=("arbitrary", "arbitrary", "arbitrary"),
                             vmem_limit_bytes=VMEM_LIMIT)


def _rms(x):
    return x * lax.rsqrt(jnp.mean(x * x, axis=-1, keepdims=True) + EPS)


def _modulated(x, gain, shift, scale):
    return (_rms(x) * gain) * (1.0 + scale) + shift


def _sigmoid(x):
    return 1.0 / (1.0 + jnp.exp(-x))


def _log_sigmoid(x):
    return jnp.minimum(x, 0.0) - jnp.log1p(jnp.exp(-jnp.abs(x)))


def _dot(a, b):
    return jnp.dot(a, b, preferred_element_type=F32)


def _dot_nt(a, b):
    return lax.dot_general(a, b, (((1,), (1,)), ((), ())), preferred_element_type=F32)


def _prefix_sum(x, axis):
    pos = lax.broadcasted_iota(jnp.int32, x.shape, axis)
    k = 1
    while k < x.shape[axis]:
        x = x + jnp.where(pos >= k, pltpu.roll(x, k, axis), 0.0)
        k *= 2
    return x


def _resident_spec(a):
    return pl.BlockSpec(a.shape, lambda *_: (0,) * a.ndim, pipeline_mode=pl.Buffered(1))


def _resident_layer_spec(a, layer):
    return pl.BlockSpec((None,) + a.shape[1:], lambda *_: (layer,) + (0,) * (a.ndim - 1),
                        pipeline_mode=pl.Buffered(1))


def _ctx_spec(tile, nt, width):
    return pl.BlockSpec((None, tile, width), lambda g, t: (0, jnp.where(g == 0, t, nt - 1), 0))


def _lat_spec(tile, width):
    return pl.BlockSpec((None, tile, width),
                        lambda g, t: (jnp.maximum(g - 1, 0), jnp.where(g > 0, t, 0), 0))


def _mod_kernel(cond_ref, w_ref, b_ref, o_ref):
    c = cond_ref[...]
    s = (c * _sigmoid(c)).astype(BF16)
    o_ref[...] = _dot(s, w_ref[...].astype(BF16)) + b_ref[...]


def _adaln_all(cond8, w_mod, b_mod):
    depth, d, n = w_mod.shape
    tn = 1024
    return pl.pallas_call(
        _mod_kernel,
        grid=(depth, n // tn),
        in_specs=[
            pl.BlockSpec((8, d), lambda l, j: (0, 0)),
            pl.BlockSpec((None, d, tn), lambda l, j: (l, 0, j)),
            pl.BlockSpec((None, 1, tn), lambda l, j: (l, 0, j)),
        ],
        out_specs=pl.BlockSpec((None, 8, tn), lambda l, j: (l, 0, j)),
        out_shape=jax.ShapeDtypeStruct((depth, 8, n), F32),
        compiler_params=_ARB2,
        name="adaln_mod",
    )(cond8, w_mod, b_mod.reshape(depth, 1, n))


def _ab_in_kernel(xp_ref, xs_ref, mod_ref, g_ref, w_ref, wqvt_ref,
                  wg_ref, wgft_ref, bg_ref, bgft_ref,
                  u_ref, qt_ref, k_ref, vt_ref, o_ref, gi_ref, gf_ref, gft_ref):
    x = jnp.where(pl.program_id(0) == 0, xp_ref[...], xs_ref[...])
    h = _modulated(x, g_ref[...], mod_ref[0:1, :], mod_ref[1:2, :])
    hb = h.astype(BF16)

    def proj(c):
        return _dot(hb, w_ref[:, c * CONV_CH:(c + 1) * CONV_CH])

    u_ref[...] = proj(0) * _sigmoid(proj(1))
    k_ref[...] = proj(3) * (M_DK ** -0.5)
    o_ref[...] = proj(5)
    qvt = _dot_nt(wqvt_ref[...], hb).astype(BF16)
    qt_ref[...] = qvt[:M_WIDTH]
    vt_ref[...] = qvt[M_WIDTH:]
    lane = lax.broadcasted_iota(jnp.int32, (x.shape[0], 128), 1)
    gates = _dot(hb, wg_ref[...]) + bg_ref[...]
    gi_ref[...] = gates[:, :128]
    gf_ref[...] = jnp.where(lane < N_STREAMS, _log_sigmoid(gates[:, 128:]), 0.0)
    gft_ref[...] = _log_sigmoid(_dot_nt(wgft_ref[...], hb) + bgft_ref[...])


def _ab_in(xp, xs, mod, gain, w_main, wqvt, wg, wgft, bg, bgft):
    tm = TOKEN_TILE
    nt = GROUP_TOKENS // tm
    tok = lambda width: pl.BlockSpec((None, tm, width), lambda g, t: (g, t, 0))
    feat = lambda rows: pl.BlockSpec((None, rows, tm), lambda g, t: (g, 0, t))
    const = _resident_spec
    tok_sds = lambda width: jax.ShapeDtypeStruct((N_GROUPS, GROUP_TOKENS, width), F32)
    feat_sds = lambda rows, dt: jax.ShapeDtypeStruct((N_GROUPS, rows, GROUP_TOKENS), dt)
    consts = (gain, w_main, wqvt, wg, wgft, bg, bgft)
    return pl.pallas_call(
        _ab_in_kernel,
        grid=(N_GROUPS, nt),
        in_specs=[_ctx_spec(tm, nt, D_MODEL), _lat_spec(tm, D_MODEL),
                  pl.BlockSpec((None, 6, D_MODEL), lambda g, t: (g, 0, 0))]
                 + [const(a) for a in consts],
        out_specs=[tok(CONV_CH), feat(M_WIDTH), tok(M_WIDTH), feat(M_WIDTH), tok(M_WIDTH),
                   tok(128), tok(128), feat(N_STREAMS)],
        out_shape=[tok_sds(CONV_CH), feat_sds(M_WIDTH, BF16), tok_sds(M_WIDTH),
                   feat_sds(M_WIDTH, BF16), tok_sds(M_WIDTH),
                   tok_sds(128), tok_sds(128), feat_sds(N_STREAMS, F32)],
        compiler_params=_ARB2,
        name="ab_in_proj",
    )(xp, xs, mod, *consts)


def _mlstm_direction(qt_ref, k_ref, vt_ref, gi_ref, gf_ref, gft_ref, h_ref, cn_s, m_s, d,
                     side_jobs):
    L = k_ref.shape[0]
    r = lax.broadcasted_iota(jnp.int32, (L, L), 0)
    c = lax.broadcasted_iota(jnp.int32, (L, L), 1)
    le = r <= c
    ge = r >= c
    gf = gf_ref[...]
    gft = gft_ref[...]
    p_col = _prefix_sum(gf, 0)
    p_row = _prefix_sum(gft, 1)
    tot_col = p_col[L - 1:L, :]
    if d == 0:
        b_col, b_row, mask = p_col, p_row, le
    else:
        b_col = tot_col - p_col + gf
        b_row = p_row[:, L - 1:L] - p_row + gft
        mask = ge
    c_all = gi_ref[...] - b_col
    m_prev_all = m_s[...]
    m_fin = jnp.maximum(m_prev_all, jnp.max(c_all, axis=0, keepdims=True))
    w_all = jnp.exp(c_all - m_fin)
    a_all = jnp.exp(m_prev_all - m_fin)
    lane = lax.broadcasted_iota(jnp.int32, (1, 128), 1)
    mine = (lane >= d * M_HEADS) & (lane < (d + 1) * M_HEADS)
    m_s[...] = jnp.where(mine, tot_col + m_fin, m_prev_all)
    ones_row = jnp.where(lax.broadcasted_iota(jnp.int32, (STATE_ROWS - M_DK, L), 0) == 0,
                         1.0, 0.0).astype(BF16)
    for h in range(M_HEADS):
        s_idx = d * M_HEADS + h
        sl = slice(h * M_DK, (h + 1) * M_DK)
        c_col = c_all[:, s_idx:s_idx + 1]
        m_prev = m_prev_all[:, s_idx:s_idx + 1]
        cm = jnp.where(mask, c_col, -jnp.inf)
        m_row = jnp.maximum(jnp.max(cm, axis=0, keepdims=True), m_prev)
        dmat = jnp.exp(cm - m_row)
        k = k_ref[:, sl]
        cn = cn_s[s_idx]
        lhs = jnp.concatenate([k.astype(BF16), cn.astype(BF16)], axis=0)
        r1 = _dot(lhs, qt_ref[sl, :])
        st = (r1[0:L] * dmat).astype(BF16)
        vt_aug = jnp.concatenate([vt_ref[sl, :], ones_row], axis=0)
        w_inter = jnp.exp(m_prev - m_row)
        tot = _dot(vt_aug, st) + w_inter * r1[L:L + STATE_ROWS]
        den = tot[M_DK:M_DK + 1, :]
        m_t = b_row[s_idx:s_idx + 1, :] + m_row
        ht = tot[0:M_DK, :] / jnp.maximum(jnp.abs(den), jnp.exp(-m_t))
        h_ref[:, sl] = ht.T
        kw = (k * w_all[:, s_idx:s_idx + 1]).astype(BF16)
        cn_s[s_idx] = a_all[:, s_idx:s_idx + 1] * cn + _dot(vt_aug, kw)
        side_jobs.pop(0)()


def _mlstm_kernel(qtf_ref, kf_ref, vtf_ref, gif_ref, gff_ref, gftf_ref,
                  qtb_ref, kb_ref, vtb_ref, gib_ref, gfb_ref, gftb_ref,
                  c0_ref, n0_ref, m0_ref,
                  up_ref, uc_ref, un_ref, cw_ref, cb_ref, lg_ref, lb_ref,
                  hf_ref, hb_ref, co_ref, no_ref, mo_ref, uconv_ref,
                  cn_s, m_s, ext_ref):
    g = pl.program_id(0)
    c = pl.program_id(1)


    @pl.when(g == 0)
    def _():
        cn_s[...] = jnp.zeros_like(cn_s)
        m_s[...] = jnp.zeros_like(m_s)

    @pl.when((g > 0) & (c == 0))
    def _():
        cn_s[...] = jnp.zeros_like(cn_s)
        for s in range(N_STREAMS):
            cn_s[s, 0:M_DK, :] = c0_ref[s].T
            cn_s[s, M_DK:M_DK + 1, :] = n0_ref[s:s + 1, :]
        m_s[...] = m0_ref[...]

    jobs = _conv_jobs(g, c, pl.num_programs(1), up_ref, uc_ref, un_ref, cw_ref, cb_ref,
                      lg_ref, lb_ref, ext_ref, uconv_ref, N_STREAMS)
    jobs.pop(0)()
    _mlstm_direction(qtf_ref, kf_ref, vtf_ref, gif_ref, gff_ref, gftf_ref, hf_ref, cn_s, m_s, 0,
                     jobs)
    _mlstm_direction(qtb_ref, kb_ref, vtb_ref, gib_ref, gfb_ref, gftb_ref, hb_ref, cn_s, m_s, 1,
                     jobs)

    @pl.when(g == 0)
    def _():
        for s in range(N_STREAMS):
            co_ref[s] = cn_s[s, 0:M_DK, :].T
            no_ref[s:s + 1, :] = cn_s[s, M_DK:M_DK + 1, :]
        mo_ref[...] = m_s[...]


def _mlstm(qt, k, vt, gi, gf, gft, c0, n0, m0, u, conv_w, conv_b, ln_g, ln_b):
    L = CHUNK
    nc = GROUP_TOKENS // L
    n_ctx = GROUP_TOKENS // CTX_SEQ
    assert CTX_SEQ == L and n_ctx == nc
    per = L // HALO
    nh = GROUP_TOKENS // HALO
    conv_consts = (conv_w, conv_b, ln_g, ln_b)
    fwd = lambda g, c: c
    bwd = lambda g, c: jnp.where(g == 0, c, nc - 1 - c)
    tok = lambda width, pos: pl.BlockSpec((None, L, width), lambda g, c: (g, pos(g, c), 0))
    feat = lambda rows, pos: pl.BlockSpec((None, rows, L), lambda g, c: (g, 0, pos(g, c)))
    lat = lambda g, c: jnp.maximum(g - 1, 0)
    ctx = lambda g, c: jnp.where(g == 0, c, n_ctx - 1)
    per_dir = lambda pos: [feat(M_WIDTH, pos), tok(M_WIDTH, pos), feat(M_WIDTH, pos),
                           tok(128, pos), tok(128, pos), feat(N_STREAMS, pos)]
    return pl.pallas_call(
        _mlstm_kernel,
        grid=(N_GROUPS, nc),
        in_specs=per_dir(fwd) + per_dir(bwd) + [
            pl.BlockSpec((None, N_STREAMS, M_DK, M_DK), lambda g, c: (lat(g, c), 0, 0, 0)),
            pl.BlockSpec((None, N_STREAMS, M_DK), lambda g, c: (lat(g, c), 0, 0)),
            pl.BlockSpec((None, 1, 128), lambda g, c: (lat(g, c), 0, 0)),
            pl.BlockSpec((None, HALO, CONV_CH),
                         lambda g, c: (g, jnp.maximum(c * per - 1, 0), 0)),
            tok(CONV_CH, fwd),
            pl.BlockSpec((None, HALO, CONV_CH),
                         lambda g, c: (g, jnp.minimum((c + 1) * per, nh - 1), 0)),
        ] + [_resident_spec(a) for a in conv_consts],
        out_specs=[
            tok(M_WIDTH, fwd), tok(M_WIDTH, bwd),
            pl.BlockSpec((None, N_STREAMS, M_DK, M_DK), lambda g, c: (ctx(g, c), 0, 0, 0)),
            pl.BlockSpec((None, N_STREAMS, M_DK), lambda g, c: (ctx(g, c), 0, 0)),
            pl.BlockSpec((None, 1, 128), lambda g, c: (ctx(g, c), 0, 0)),
            tok(CONV_CH, fwd),
        ],
        out_shape=[
            jax.ShapeDtypeStruct((N_GROUPS, GROUP_TOKENS, M_WIDTH), F32),
            jax.ShapeDtypeStruct((N_GROUPS, GROUP_TOKENS, M_WIDTH), F32),
            jax.ShapeDtypeStruct((n_ctx, N_STREAMS, M_DK, M_DK), F32),
            jax.ShapeDtypeStruct((n_ctx, N_STREAMS, M_DK), F32),
            jax.ShapeDtypeStruct((n_ctx, 1, 128), F32),
            jax.ShapeDtypeStruct((N_GROUPS, GROUP_TOKENS, CONV_CH), BF16),
        ],
        scratch_shapes=[
            pltpu.VMEM((N_STREAMS, STATE_ROWS, M_DK), F32),
            pltpu.VMEM((1, 128), F32),
            pltpu.VMEM((SUBLANES, L + 2 * HALO, CONV_CH), F32),
        ],
        compiler_params=_ARB2,
        name="mlstm_scan",
    )(qt, k, vt, gi, gf, gft, qt, k, vt, gi, gf, gft, c0, n0, m0, u, u, u, *conv_consts)


def _conv_jobs(g, t, nt, up_ref, uc_ref, un_ref, cw_ref, cb_ref, lg_ref, lb_ref,
               ext_ref, out_ref, n_blocks):
    T = uc_ref.shape[0]
    rows = T // n_blocks
    span = T + 2 * HALO - SUBLANES
    off = HALO - CONV_WIDTH // 2

    def prepare():
        prev_on = jnp.where((g > 0) & (t > 0), 1.0, 0.0).astype(F32)
        next_on = jnp.where((g > 0) & (t < nt - 1), 1.0, 0.0).astype(F32)
        ext_ref[0, 0:HALO, :] = up_ref[...] * prev_on
        ext_ref[0, HALO:HALO + T, :] = uc_ref[...]
        ext_ref[0, HALO + T:HALO + T + HALO, :] = un_ref[...] * next_on
        for s in range(1, SUBLANES):
            ext_ref[s, 0:span, :] = ext_ref[0, s:s + span, :]

    def block(r0):
        def run():
            groups = rows // SUBLANES
            accs = [cb_ref[...]] * groups
            for j in range(CONV_WIDTH):
                phase, base = (off + j) % SUBLANES, (off + j) // SUBLANES * SUBLANES
                w = cw_ref[j]
                for q in range(groups):
                    lo = r0 + base + q * SUBLANES
                    accs[q] = accs[q] + w * ext_ref[phase, lo:lo + SUBLANES, :]
            y = jnp.concatenate(accs, axis=0)
            mu = jnp.mean(y, axis=-1, keepdims=True)
            yc = y - mu
            var = jnp.mean(yc * yc, axis=-1, keepdims=True)
            y = yc * lax.rsqrt(var + EPS) * lg_ref[...] + lb_ref[...]
            out_ref[r0:r0 + rows, :] = (y * _sigmoid(y)).astype(BF16)
        return run

    return [prepare] + [block(r0) for r0 in range(0, T, rows)]


def _mix_stage(hf_ref, hb_ref, o_ref, mg_ref):
    hm = hf_ref[...] + hb_ref[...]
    hm = jnp.concatenate(
        [_rms(hm[:, h * M_DK:(h + 1) * M_DK]) for h in range(M_HEADS)], axis=-1)
    return (_sigmoid(o_ref[...]) * (hm * mg_ref[...])).astype(BF16)


def _ffn_tail(x, mod_ref, g_ref, win_ref, wout_ref):
    hb = _modulated(x, g_ref[...], mod_ref[3:4, :], mod_ref[4:5, :]).astype(BF16)
    acc = jnp.zeros(x.shape, F32)
    for lo, hi in zip(FFN_CUTS[:-1], FFN_CUTS[1:]):
        gt = _dot(hb, win_ref[:, lo:hi])
        up = _dot(hb, win_ref[:, D_FF + lo:D_FF + hi])
        act = (gt * _sigmoid(gt) * up).astype(BF16)
        acc = acc + _dot(act, wout_ref[lo:hi, :])
    return x + mod_ref[5:6, :] * acc


def _l0_tail_kernel(u_ref, hf_ref, hb_ref, o_ref, xp_ref, xs_ref, mod_ref,
                    mg_ref, wu_ref, wm_ref, g2_ref, win_ref, wout_ref, out_ref):
    hm = _mix_stage(hf_ref, hb_ref, o_ref, mg_ref)
    x = jnp.where(pl.program_id(0) == 0, xp_ref[...], xs_ref[...])
    out = _dot(u_ref[...], wu_ref[...]) + _dot(hm, wm_ref[...])
    out_ref[...] = _ffn_tail(x + mod_ref[2:3, :] * out, mod_ref, g2_ref, win_ref, wout_ref)


def _l0_tail(u_conv, hf, hb, o, xp, xs, mod, mnorm_g, w_u, w_m, gain2, w_in_all, w_out_all,
             layer):
    T = TOKEN_TILE
    nt = GROUP_TOKENS // T
    tok = lambda width: pl.BlockSpec((None, T, width), lambda g, t: (g, t, 0))
    consts = (mnorm_g, w_u, w_m, gain2)
    stacked = (w_in_all, w_out_all)
    return pl.pallas_call(
        _l0_tail_kernel,
        grid=(N_GROUPS, nt),
        in_specs=[
            tok(CONV_CH), tok(M_WIDTH), tok(M_WIDTH), tok(M_WIDTH),
            _ctx_spec(T, nt, D_MODEL), _lat_spec(T, D_MODEL),
            pl.BlockSpec((None, 6, D_MODEL), lambda g, t: (g, 0, 0)),
        ] + [_resident_spec(a) for a in consts]
          + [_resident_layer_spec(a, layer) for a in stacked],
        out_specs=tok(D_MODEL),
        out_shape=jax.ShapeDtypeStruct((N_GROUPS, GROUP_TOKENS, D_MODEL), F32),
        compiler_params=_ARB2,
        name="l0_mix_ffn",
    )(u_conv, hf, hb, o, xp, xs, mod, *consts, *stacked)


def _l1_tail_kernel(o_ref, x_ref, mod_ref, wc_ref, g2_ref, win_ref, wout_ref, fg_ref,
                    yp_ref, ys_ref):
    x = x_ref[...] + mod_ref[2:3, :] * _dot(o_ref[...], wc_ref[...])
    y = _rms(_ffn_tail(x, mod_ref, g2_ref, win_ref, wout_ref)) * fg_ref[...]
    g = pl.program_id(0)

    @pl.when(g == 0)
    def _():
        yp_ref[...] = y

    @pl.when(g > 0)
    def _():
        ys_ref[...] = y


def _l1_tail(o, x, mod, w_c, gain2, w_in_all, w_out_all, layer, final_gain):
    tm = TOKEN_TILE
    nt = GROUP_TOKENS // tm
    tok = pl.BlockSpec((None, tm, D_MODEL), lambda g, t: (g, t, 0))
    consts = (w_c, gain2, w_in_all, w_out_all, final_gain)
    const_specs = [_resident_spec(w_c), _resident_spec(gain2),
                   _resident_layer_spec(w_in_all, layer), _resident_layer_spec(w_out_all, layer),
                   _resident_spec(final_gain)]
    return pl.pallas_call(
        _l1_tail_kernel,
        grid=(N_GROUPS, nt),
        in_specs=[tok, tok, pl.BlockSpec((None, 6, D_MODEL), lambda g, t: (g, 0, 0))]
                 + const_specs,
        out_specs=[_ctx_spec(tm, nt, D_MODEL), _lat_spec(tm, D_MODEL)],
        out_shape=[jax.ShapeDtypeStruct((1, GROUP_TOKENS, D_MODEL), F32),
                   jax.ShapeDtypeStruct((N_GROUPS - 1, GROUP_TOKENS, D_MODEL), F32)],
        compiler_params=_ARB2,
        name="l1_proj_ffn",
    )(o, x, mod, *consts)


def _paired_layout(a):
    quarter = HEAD_DIM // 4
    b = a.reshape(a.shape[:-1] + (a.shape[-1] // HEAD_DIM, 4, quarter))
    b = jnp.stack([b[..., 0, :], b[..., 2, :], b[..., 1, :], b[..., 3, :]], axis=-2)
    return b.reshape(a.shape)


def _rope(x, cos, sin):
    return x * cos + pltpu.roll(x, HEAD_DIM // 2, 1) * sin


def _c_in_kernel(x_ref, mod_ref, g_ref, w_ref, qg_ref, kg_ref, cos_ref, sin_ref,
                 q_ref, k_ref, v_ref, kn_ref, vn_ref):
    hb = _modulated(x_ref[...], g_ref[...], mod_ref[0:1, :], mod_ref[1:2, :]).astype(BF16)
    latent = pl.program_id(0) > 0
    cos = jnp.where(latent, cos_ref[...], 1.0)
    sin = jnp.where(latent, sin_ref[...], 0.0)
    q = _dot(hb, w_ref[:, 0:N_Q * HEAD_DIM])
    qg = qg_ref[...] * (HEAD_DIM ** -0.5 * LOG2_E)
    for h in range(N_Q):
        lanes = slice(h * HEAD_DIM, (h + 1) * HEAD_DIM)
        q_ref[:, lanes] = _rope(_rms(q[:, lanes]) * qg, cos, sin).astype(BF16)
    kv = _dot(hb, w_ref[:, N_Q * HEAD_DIM:(N_Q + 2 * N_KV) * HEAD_DIM])
    kg = kg_ref[...]
    kn = [_rms(kv[:, h * HEAD_DIM:(h + 1) * HEAD_DIM]) * kg for h in range(N_KV)]
    for h in range(N_KV):
        k_ref[:, h * HEAD_DIM:(h + 1) * HEAD_DIM] = _rope(kn[h], cos, sin).astype(BF16)
    v = kv[:, N_KV * HEAD_DIM:]
    v_ref[...] = v.astype(BF16)

    @pl.when(pl.program_id(0) == 0)
    def _():
        for h in range(N_KV):
            kn_ref[:, h * HEAD_DIM:(h + 1) * HEAD_DIM] = kn[h]
        vn_ref[...] = v


def _c_in(x, mod, gain, w, qg, kg, cos, sin):
    tm = ROPE_TILE
    kvw = N_KV * HEAD_DIM
    tok = lambda width: pl.BlockSpec((None, tm, width), lambda g, t: (g, t, 0))
    const = _resident_spec
    table = pl.BlockSpec((tm, HEAD_DIM), lambda g, t: (t, 0))
    nt = GROUP_TOKENS // tm
    ctx_only = pl.BlockSpec((tm, kvw), lambda g, t: (jnp.where(g == 0, t, nt - 1), 0))
    sds = lambda width, dt: jax.ShapeDtypeStruct((N_GROUPS, GROUP_TOKENS, width), dt)
    return pl.pallas_call(
        _c_in_kernel,
        grid=(N_GROUPS, GROUP_TOKENS // tm),
        in_specs=[tok(D_MODEL), pl.BlockSpec((None, 6, D_MODEL), lambda g, t: (g, 0, 0)),
                  const(gain), const(w), const(qg), const(kg), table, table],
        out_specs=[tok(N_Q * HEAD_DIM), tok(kvw), tok(kvw), ctx_only, ctx_only],
        out_shape=[sds(N_Q * HEAD_DIM, BF16), sds(kvw, BF16), sds(kvw, BF16),
                   jax.ShapeDtypeStruct((GROUP_TOKENS, kvw), F32),
                   jax.ShapeDtypeStruct((GROUP_TOKENS, kvw), F32)],
        compiler_params=_ARB2,
        name="attn_in_proj",
    )(x, mod, gain, w, qg, kg, cos, sin)


def _group_softmax_pv(q_ref, o_ref, rows, k, v_aug):
    for i in range(GROUP):
        lanes = slice(i * HEAD_DIM, (i + 1) * HEAD_DIM)
        s = _dot_nt(q_ref[rows, lanes], k() if callable(k) else k)
        p = jnp.exp2(s - jnp.max(s, axis=-1, keepdims=True)).astype(BF16)
        o = _dot(p, v_aug() if callable(v_aug) else v_aug)
        o_ref[rows, lanes] = (o[:, :HEAD_DIM] / o[:, HEAD_DIM:HEAD_DIM + 1]).astype(BF16)


def _ones_column(rows):
    lane = lax.broadcasted_iota(jnp.int32, (rows, HEAD_DIM), 1)
    return jnp.where(lane == 0, 1.0, 0.0).astype(BF16)


def _attn_kernel(q_ref, k_ref, v_ref, kc_ref, vc_ref, o_ref, kall_ref, vall_ref):
    g = pl.program_id(0)
    t = pl.program_id(2)
    T = q_ref.shape[0]
    S = k_ref.shape[0]

    @pl.when(g == 0)
    def _():
        for j in range(T // CTX_SEQ):
            keys = pl.ds(pl.multiple_of(t * T + j * CTX_SEQ, CTX_SEQ), CTX_SEQ)
            v_aug = jnp.concatenate([v_ref[keys, :], _ones_column(CTX_SEQ)], axis=-1)
            _group_softmax_pv(q_ref, o_ref, slice(j * CTX_SEQ, (j + 1) * CTX_SEQ),
                              k_ref[keys, :], v_aug)

    @pl.when((g > 0) & (t == 0))
    def _():
        kall_ref[0:S, :] = k_ref[...]
        kall_ref[S:, :] = kc_ref[...]
        vall_ref[0:S, 0:HEAD_DIM] = v_ref[...]
        vall_ref[S:, 0:HEAD_DIM] = vc_ref[...]
        vall_ref[:, HEAD_DIM:] = _ones_column(vall_ref.shape[0])

    @pl.when(g > 0)
    def _():
        for r0 in range(0, T, ATTN_CHAIN_ROWS):
            _group_softmax_pv(q_ref, o_ref, slice(r0, r0 + ATTN_CHAIN_ROWS),
                              lambda: kall_ref[...], lambda: vall_ref[...])


def _attention(q, k, v, cache_k, cache_v):
    T = ATTN_TILE
    assert T % CTX_SEQ == 0
    gw = GROUP * HEAD_DIM
    qspec = pl.BlockSpec((None, T, gw), lambda g, h, t: (g, t, h))
    kspec = pl.BlockSpec((None, GROUP_TOKENS, HEAD_DIM), lambda g, h, t: (g, 0, h))
    cspec = pl.BlockSpec((None, PAST_LEN, HEAD_DIM),
                         lambda g, h, t: (jnp.maximum(g - 1, 0), 0, h))
    n_keys = GROUP_TOKENS + PAST_LEN
    return pl.pallas_call(
        _attn_kernel,
        grid=(N_GROUPS, N_KV, GROUP_TOKENS // T),
        in_specs=[qspec, kspec, kspec, cspec, cspec],
        out_specs=qspec,
        out_shape=jax.ShapeDtypeStruct(q.shape, BF16),
        scratch_shapes=[pltpu.VMEM((n_keys, HEAD_DIM), BF16),
                        pltpu.VMEM((n_keys, 2 * HEAD_DIM), BF16)],
        compiler_params=_ARB3,
        name="gqa_attention",
    )(q, k, v, cache_k, cache_v)


def _rope_tables():
    rows = GROUP_TOKENS // GRID_W
    assert rows == GRID_W
    n_freq = HEAD_DIM // 4
    inv = 1.0 / (ROPE_THETA ** (jnp.arange(n_freq, dtype=F32) / n_freq))
    ang = jnp.arange(GRID_W).astype(F32)[:, None] * inv
    by_row = lambda a: jnp.repeat(a, GRID_W, axis=0)
    by_col = lambda a: jnp.tile(a, (rows, 1))
    cos_r, cos_c = by_row(jnp.cos(ang)), by_col(jnp.cos(ang))
    sin_r, sin_c = by_row(jnp.sin(ang)), by_col(jnp.sin(ang))
    cos = jnp.concatenate([cos_r, cos_c, cos_r, cos_c], axis=-1)
    sin = jnp.concatenate([-sin_r, -sin_c, sin_r, sin_c], axis=-1)
    return cos, sin


def kernel(x_prompt, x_sample, c, state_mlstm_C, state_mlstm_n, state_mlstm_m, cache_k, cache_v,
           c_ctx, w_mod, b_mod, norm1_g, norm2_g, w_in_ab, b_gate_ab, conv_w, conv_b, conv_ln_g,
           conv_ln_b, mlstm_norm_g, w_out_ab, w_in_c, q_norm_g, k_norm_g, w_out_c, w_ffn_in,
           w_ffn_out, final_norm_g):
    n_lat = x_sample.shape[0]
    n_ctx = x_prompt.shape[0]
    xp = x_prompt.reshape(1, GROUP_TOKENS, D_MODEL)

    cond8 = jnp.concatenate([c_ctx[None, :], c, jnp.zeros((8 - 1 - n_lat, D_MODEL), F32)], axis=0)
    mods = _adaln_all(cond8, w_mod, b_mod)[:, :N_GROUPS].reshape(2, N_GROUPS, 6, D_MODEL)
    row = lambda a: a.reshape(1, -1)
    w_ffn_in_b = w_ffn_in.astype(BF16)
    w_ffn_out_b = w_ffn_out.astype(BF16)

    main_cols = 2 * CONV_CH + 4 * M_WIDTH
    w_ab = w_in_ab[0]
    w_main = w_ab[:, :main_cols].astype(BF16)
    q_cols = slice(2 * CONV_CH, 2 * CONV_CH + M_WIDTH)
    v_cols = slice(2 * CONV_CH + 2 * M_WIDTH, 2 * CONV_CH + 3 * M_WIDTH)
    wqvt = jnp.concatenate([w_ab[:, q_cols], w_ab[:, v_cols]], axis=1).T.astype(BF16)
    gate_w = w_ab[:, main_cols:].reshape(D_MODEL, 2, 2, M_HEADS)
    gate_b = b_gate_ab[0].reshape(2, 2, M_HEADS)
    pad = 128 - N_STREAMS
    lanes = lambda a: jnp.pad(a.reshape(-1, N_STREAMS), ((0, 0), (0, pad)))
    wg = jnp.concatenate([lanes(gate_w[:, :, 0]), lanes(gate_w[:, :, 1])], axis=1).astype(BF16)
    bg = jnp.concatenate([lanes(gate_b[:, 0]), lanes(gate_b[:, 1])], axis=1)
    wgft = gate_w[:, :, 1].reshape(D_MODEL, N_STREAMS).T.astype(BF16)
    bgft = gate_b[:, 1].reshape(N_STREAMS, 1)
    u, qt, k, vt, o, gi, gf, gft = _ab_in(xp, x_sample, mods[0], row(norm1_g[0]), w_main, wqvt,
                                          wg, wgft, bg, bgft)

    c0 = state_mlstm_C[:, 0].reshape(n_lat, N_STREAMS, M_DK, M_DK)
    n0 = state_mlstm_n[:, 0].reshape(n_lat, N_STREAMS, M_DK)
    m0 = jnp.pad(state_mlstm_m[:, 0].reshape(n_lat, N_STREAMS), ((0, 0), (0, pad)))
    conv_w_rep = jnp.broadcast_to(conv_w[0][:, None, :], (CONV_WIDTH, SUBLANES, CONV_CH))
    conv_b_rep = jnp.broadcast_to(conv_b[0][None, :], (SUBLANES, CONV_CH))
    hf, hb, new_c, new_n, new_m, u_conv = _mlstm(
        qt, k, vt, gi, gf, gft, c0, n0, m0.reshape(n_lat, 1, 128),
        u, conv_w_rep, conv_b_rep, row(conv_ln_g[0]), row(conv_ln_b[0]))

    w_o = w_out_ab[0].astype(BF16)
    x = _l0_tail(u_conv, hf, hb, o, xp, x_sample, mods[0], row(mlstm_norm_g[0]),
                 w_o[:CONV_CH], w_o[CONV_CH:], row(norm2_g[0]),
                 w_ffn_in_b, w_ffn_out_b, 0)

    cos, sin = _rope_tables()
    kvw = N_KV * HEAD_DIM
    qk_cols = (N_Q + N_KV) * HEAD_DIM
    w_c = jnp.concatenate([_paired_layout(w_in_c[0][:, :qk_cols]), w_in_c[0][:, qk_cols:]],
                          axis=1).astype(BF16)
    qa, ka, va, k_norm, v_raw = _c_in(x, mods[1], row(norm1_g[1]), w_c,
                                      _paired_layout(row(q_norm_g[0])),
                                      _paired_layout(row(k_norm_g[0])), cos, sin)
    k_norm = _paired_layout(k_norm)
    ck = _paired_layout(cache_k[:, 0].reshape(n_lat, PAST_LEN, kvw)).astype(BF16)
    cv = cache_v[:, 0].reshape(n_lat, PAST_LEN, kvw).astype(BF16)
    oa = _attention(qa, ka, va, ck, cv)
    yp, y_sample = _l1_tail(oa, x, mods[1], w_out_c[0].astype(BF16), row(norm2_g[1]),
                            w_ffn_in_b, w_ffn_out_b, 1, row(final_norm_g))

    y_prompt = yp.reshape(n_ctx, CTX_SEQ, D_MODEL)
    new_state_c = new_c.reshape(n_ctx, 1, 2, M_HEADS, M_DK, M_DK)
    new_state_n = new_n.reshape(n_ctx, 1, 2, M_HEADS, M_DK)
    new_state_m = new_m[:, 0, :N_STREAMS].reshape(n_ctx, 1, 2, M_HEADS)
    new_k = k_norm.reshape(n_ctx, 1, CTX_SEQ, N_KV, HEAD_DIM)
    new_v = v_raw.reshape(n_ctx, 1, CTX_SEQ, N_KV, HEAD_DIM)
    return (y_prompt, y_sample, new_state_c, new_state_n, new_state_m, new_k, new_v)
```
